```python
import math
import jax, jax.numpy as jnp
from jax import lax
import numpy as np

D_MODEL = 1024
BATCH = 16
SEQ = 2048
DEPTH = 2
DEC_BATCH = 8
DEC_SEQ = 8192
PAST_LEN = 128

N_HEADS = 16
N_KV_HEADS = 4
HEAD_DIM = D_MODEL // N_HEADS
Q_WIDTH = N_HEADS * HEAD_DIM
KV_WIDTH = N_KV_HEADS * HEAD_DIM
QKV_WIDTH = Q_WIDTH + 2 * KV_WIDTH
A_HALF_WINDOW = 128
A_BLOCK = 128
DILATED_GROUPS = ((128, 1), (512, 4), (2048, 16))
B_BLOCK = 64
NUM_BUCKETS = 32
MAX_DISTANCE = 1024
D_FF = ((8 * D_MODEL // 3 + 255) // 256) * 256
N_A_LAYERS = (DEPTH + 1) // 2
N_B_LAYERS = DEPTH // 2
EPS = 1e-6
NEG = -1e30

kernel_name = "hybrid_windowed_dilated_encoder"


def rms_norm(x, g):
    xf = x.astype(jnp.float32)
    y = xf * lax.rsqrt(jnp.mean(xf * xf, axis=-1, keepdims=True) + EPS)
    return (y * g.astype(jnp.float32)).astype(x.dtype)


def t5_buckets(rel):
    nb = NUM_BUCKETS // 2
    max_exact = nb // 2
    n = np.abs(rel)
    large = max_exact + (np.log(np.maximum(n, 1) / max_exact)
                         / math.log(MAX_DISTANCE / max_exact) * (nb - max_exact)).astype(np.int32)
    large = np.minimum(large, nb - 1)
    return ((rel > 0).astype(np.int32) * nb + np.where(n < max_exact, n, large)).astype(np.int32)


def relative_bias(rel_table, blk, dil):
    qi = np.arange(blk)[:, None]
    kj = np.arange(3 * blk)[None, :]
    buckets = t5_buckets(dil * (kj - blk - qi))
    return jnp.take(rel_table, jnp.asarray(buckets), axis=0).transpose(2, 0, 1).astype(jnp.float32)


def banded_attention(q, k, v, bias, blk, half_window, sink):
    N, L, H, Dh = q.shape
    G = k.shape[2]
    R = H // G
    nb = -(-L // blk)
    pad = nb * blk - L
    qp = jnp.pad(q, ((0, 0), (0, pad), (0, 0), (0, 0)))
    kp = jnp.pad(k, ((0, 0), (blk, blk + pad), (0, 0), (0, 0)))
    vp = jnp.pad(v, ((0, 0), (blk, blk + pad), (0, 0), (0, 0)))
    rel = np.arange(3 * blk)[None, :] - blk - np.arange(blk)[:, None]
    band = jnp.asarray(np.abs(rel) <= half_window)
    bias_g = bias.reshape(G, R, blk, 3 * blk)
    scale = Dh ** -0.5

    def one_block(n):
        start = n * blk
        qb = lax.dynamic_slice_in_dim(qp, start, blk, axis=1).reshape(N, blk, G, R, Dh).astype(jnp.float32)
        kb = lax.dynamic_slice_in_dim(kp, start, 3 * blk, axis=1).astype(jnp.float32)
        vb = lax.dynamic_slice_in_dim(vp, start, 3 * blk, axis=1).astype(jnp.float32)
        kpos = start - blk + jnp.arange(3 * blk)
        valid = jnp.logical_and(band, ((kpos >= 0) & (kpos < L))[None, :])
        s = jnp.einsum('nqgrd,nkgd->ngrqk', qb, kb) * scale + bias_g
        s = jnp.where(valid, s, NEG)
        lse = jax.nn.logsumexp(s, axis=-1)
        if sink is not None:
            lse = jnp.logaddexp(lse, sink.astype(jnp.float32).reshape(G, R)[None, :, :, None])
        p = jnp.exp(s - lse[..., None])
        o = jnp.einsum('ngrqk,nkgd->nqgrd', p, vb).reshape(N, blk, H, Dh)
        return o, lse.transpose(0, 3, 1, 2).reshape(N, blk, H)

    o, lse = lax.map(one_block, jnp.arange(nb))
    o = jnp.moveaxis(o, 0, 1).reshape(N, nb * blk, H, Dh)[:, :L]
    lse = jnp.moveaxis(lse, 0, 1).reshape(N, nb * blk, H)[:, :L]
    return o, lse


def split_heads(qkv_g, Bn, S):
    q = qkv_g[..., :Q_WIDTH].reshape(Bn, S, N_HEADS, HEAD_DIM)
    k = qkv_g[..., Q_WIDTH:Q_WIDTH + KV_WIDTH].reshape(Bn, S, N_KV_HEADS, HEAD_DIM)
    v = qkv_g[..., Q_WIDTH + KV_WIDTH:].reshape(Bn, S, N_KV_HEADS, HEAD_DIM)
    return q, k, v


def mixer_windowed(h, w_qkv, q_gain, k_gain, sink, w_o, rel_table):
    Bn, S, _ = h.shape
    q, k, v = split_heads(h @ w_qkv, Bn, S)
    q = rms_norm(q, q_gain)
    k = rms_norm(k, k_gain)
    bias = relative_bias(rel_table, A_BLOCK, 1)
    o, _ = banded_attention(q, k, v, bias, A_BLOCK, A_HALF_WINDOW, sink)
    return o.astype(h.dtype).reshape(Bn, S, Q_WIDTH) @ w_o


def mixer_dilated(h, w_qkv, q_gain, k_gain, w_o, rel_table):
    Bn, S, _ = h.shape
    qkv = h @ w_qkv
    outs, lses = [], []
    for gi, (window, dil) in enumerate(DILATED_GROUPS):
        q, k, v = split_heads(qkv[..., gi * QKV_WIDTH:(gi + 1) * QKV_WIDTH], Bn, S)
        q = rms_norm(q, q_gain[gi])
        k = rms_norm(k, k_gain[gi])
        Ls = S // dil
        fold = lambda t: t.reshape(Bn, Ls, dil, t.shape[2], HEAD_DIM).swapaxes(1, 2).reshape(Bn * dil, Ls, t.shape[2], HEAD_DIM)
        bias = relative_bias(rel_table, B_BLOCK, dil)
        o, lse = banded_attention(fold(q), fold(k), fold(v), bias, B_BLOCK, window // (2 * dil), None)
        outs.append(o.reshape(Bn, dil, Ls, N_HEADS, HEAD_DIM).swapaxes(1, 2).reshape(Bn, S, N_HEADS, HEAD_DIM))
        lses.append(lse.reshape(Bn, dil, Ls, N_HEADS).swapaxes(1, 2).reshape(Bn, S, N_HEADS))
    wts = jax.nn.softmax(jnp.stack(lses), axis=0)
    o = jnp.einsum('gbsh,gbshd->bshd', wts, jnp.stack(outs))
    return o.astype(h.dtype).reshape(Bn, S, Q_WIDTH) @ w_o


def swiglu(h, w_gate_up, w_down):
    gu = h @ w_gate_up
    return (jax.nn.silu(gu[..., :D_FF]) * gu[..., D_FF:]) @ w_down


def trunk(x, rel_table, norm_attn, norm_ffn, a_w_qkv, a_q_gain, a_k_gain, a_sink, a_w_o,
          b_w_qkv, b_q_gain, b_k_gain, b_w_o, ffn_w_gate_up, ffn_w_down):
    for i in range(DEPTH):
        h = rms_norm(x, norm_attn[i])
        j = i // 2
        if i % 2 == 0:
            x = x + mixer_windowed(h, a_w_qkv[j], a_q_gain[j], a_k_gain[j], a_sink[j], a_w_o[j], rel_table)
        else:
            x = x + mixer_dilated(h, b_w_qkv[j], b_q_gain[j], b_k_gain[j], b_w_o[j], rel_table)
        x = x + swiglu(rms_norm(x, norm_ffn[i]), ffn_w_gate_up[i], ffn_w_down[i])
    return x


def setup_inputs(seed: int = 0) -> dict:
    key = jax.random.key(seed)
    ks = jax.random.split(key, 16)
    f32 = jnp.float32
    nrm = lambda k, shape, s: jax.random.normal(k, shape, f32) * s
    n_grp = len(DILATED_GROUPS)
    return {
        "x_prompt": nrm(ks[0], (BATCH, SEQ, D_MODEL), 1.0),
        "x_sample": nrm(ks[1], (DEC_BATCH, DEC_SEQ, D_MODEL), 1.0),
        "rel_table": nrm(ks[2], (NUM_BUCKETS, N_HEADS), 0.5),
        "norm_attn": 1.0 + nrm(ks[3], (DEPTH, D_MODEL), 0.02),
        "norm_ffn": 1.0 + nrm(ks[4], (DEPTH, D_MODEL), 0.02),
        "a_w_qkv": nrm(ks[5], (N_A_LAYERS, D_MODEL, QKV_WIDTH), D_MODEL ** -0.5),
        "a_q_gain": 1.0 + nrm(ks[6], (N_A_LAYERS, HEAD_DIM), 0.02),
        "a_k_gain": 1.0 + nrm(ks[7], (N_A_LAYERS, HEAD_DIM), 0.02),
        "a_sink": nrm(ks[8], (N_A_LAYERS, N_HEADS), 1.0),
        "a_w_o": nrm(ks[9], (N_A_LAYERS, Q_WIDTH, D_MODEL), Q_WIDTH ** -0.5),
        "b_w_qkv": nrm(ks[10], (N_B_LAYERS, D_MODEL, n_grp * QKV_WIDTH), D_MODEL ** -0.5),
        "b_q_gain": 1.0 + nrm(ks[11], (N_B_LAYERS, n_grp, HEAD_DIM), 0.02),
        "b_k_gain": 1.0 + nrm(ks[12], (N_B_LAYERS, n_grp, HEAD_DIM), 0.02),
        "b_w_o": nrm(ks[13], (N_B_LAYERS, Q_WIDTH, D_MODEL), Q_WIDTH ** -0.5),
        "ffn_w_gate_up": nrm(ks[14], (DEPTH, D_MODEL, 2 * D_FF), D_MODEL ** -0.5),
        "ffn_w_down": nrm(ks[15], (DEPTH, D_FF, D_MODEL), D_FF ** -0.5),
    }


def reference(x_prompt, x_sample, rel_table, norm_attn, norm_ffn, a_w_qkv, a_q_gain, a_k_gain,
              a_sink, a_w_o, b_w_qkv, b_q_gain, b_k_gain, b_w_o, ffn_w_gate_up, ffn_w_down):
    y_prompt = trunk(x_prompt, rel_table, norm_attn, norm_ffn, a_w_qkv, a_q_gain, a_k_gain, a_sink, a_w_o,
                     b_w_qkv, b_q_gain, b_k_gain, b_w_o, ffn_w_gate_up, ffn_w_down)
    y_sample = trunk(x_sample, rel_table, norm_attn, norm_ffn, a_w_qkv, a_q_gain, a_k_gain, a_sink, a_w_o,
                     b_w_qkv, b_q_gain, b_k_gain, b_w_o, ffn_w_gate_up, ffn_w_down)
    return (y_prompt, y_sample)
```

```python
import functools
import math

import numpy as np
import jax
import jax.numpy as jnp
from jax import lax
from jax.experimental import pallas as pl
from jax.experimental.pallas import tpu as pltpu

D_MODEL = 1024
N_HEADS = 16
N_KV_HEADS = 4
HEAD_DIM = 64
Q_WIDTH = N_HEADS * HEAD_DIM
KV_WIDTH = N_KV_HEADS * HEAD_DIM
QKV_WIDTH = Q_WIDTH + 2 * KV_WIDTH
D_FF = 2816
A_HALF_WINDOW = 128
DILATED_GROUPS = ((128, 1), (512, 4), (2048, 16))
NUM_BUCKETS = 32
MAX_DISTANCE = 1024
EPS = 1e-6
NEG = -1e30

LANES = 128
MXU_TILE = 256
ATTN_BLK = 128
VMEM_LIMIT = 56 * 1024 * 1024

BF16 = jnp.bfloat16
F32 = jnp.float32


def _dot(a, b):
    return jnp.dot(a, b, preferred_element_type=F32)


def _dot_nt(a, b):
    return lax.dot_general(a, b, (((1,), (1,)), ((), ())), preferred_element_type=F32)


def _resident(shape):
    nd = len(shape)
    return pl.BlockSpec(shape, lambda *_: (0,) * nd, pipeline_mode=pl.Buffered(1))


def _qkv_kernel(x_ref, g_ref, w_ref, bd_ref, qg_ref, kg_ref, *out_refs, n_groups):
    x = x_ref[...]
    ms = jnp.mean(x * x, axis=-1, keepdims=True)
    h = (x * lax.rsqrt(ms + EPS) * g_ref[...]).astype(BF16)
    bd = bd_ref[...]

    def head_norm(t, gain):
        ss = _dot((t * t).astype(BF16), bd)
        return (t * lax.rsqrt(ss * (1.0 / HEAD_DIM) + EPS) * gain).astype(BF16)

    for g in range(n_groups):
        q_ref, k_ref, v_ref = out_refs[3 * g:3 * g + 3]
        base = g * QKV_WIDTH
        for c in range(Q_WIDTH // MXU_TILE):
            lo = c * MXU_TILE
            t = _dot(h, w_ref[:, base + lo:base + lo + MXU_TILE])
            q_ref[:, lo:lo + MXU_TILE] = head_norm(t, qg_ref[g, :, lo:lo + MXU_TILE])
        t = _dot(h, w_ref[:, base + Q_WIDTH:base + Q_WIDTH + KV_WIDTH])
        k_ref[...] = head_norm(t, kg_ref[g])
        v_ref[...] = _dot(h, w_ref[:, base + Q_WIDTH + KV_WIDTH:base + QKV_WIDTH]).astype(BF16)


def _qkv_call(x2d, gain, w, bd, qg, kg, n_groups, tm=512):
    m = x2d.shape[0]
    row = lambda width: pl.BlockSpec((tm, width), lambda i: (i, 0))
    out_shape, out_specs = [], []
    for _ in range(n_groups):
        for width in (Q_WIDTH, KV_WIDTH, KV_WIDTH):
            out_shape.append(jax.ShapeDtypeStruct((m, width), BF16))
            out_specs.append(row(width))
    return pl.pallas_call(
        functools.partial(_qkv_kernel, n_groups=n_groups),
        grid=(m // tm,),
        in_specs=[row(D_MODEL), _resident(gain.shape), _resident(w.shape), _resident(bd.shape),
                  _resident(qg.shape), _resident(kg.shape)],
        out_specs=out_specs,
        out_shape=out_shape,
        compiler_params=pltpu.CompilerParams(dimension_semantics=("parallel",),
                                             vmem_limit_bytes=VMEM_LIMIT),
        name=f"qkv{n_groups}",
    )(x2d, gain, w, bd, qg, kg)


def _attn_kernel(*refs, tq, hw, n_blocks, has_sink, has_lse):
    q_ref, kl_ref, kc_ref, kr_ref, vl_ref, vc_ref, vr_ref, bias_ref = refs[:8]
    refs = refs[8:]
    sink_ref = None
    if has_sink:
        sink_ref, refs = refs[0], refs[1:]
    o_ref, refs = refs[0], refs[1:]
    lse_ref = None
    if has_lse:
        lse_ref, refs = refs[0], refs[1:]
    kw, vw = refs

    blk = ATTN_BLK
    win = blk + 2 * hw
    rows = 4 * blk

    for dst, (l, c, r) in ((kw, (kl_ref, kc_ref, kr_ref)), (vw, (vl_ref, vc_ref, vr_ref))):
        dst[0:hw, :] = l[...]
        dst[hw:hw + tq, :] = c[...]
        dst[hw + tq:hw + tq + hw, :] = r[...]

    tile = pl.program_id(2)
    n_inner = tq // blk
    lane = lax.broadcasted_iota(jnp.int32, (blk, LANES), 1)
    keep = (lane < HEAD_DIM, lane >= HEAD_DIM)

    def body(j, carry):
        row0 = pl.multiple_of(j * blk, blk)
        gb = tile * n_inner + j
        kind = 2 * (gb == 0).astype(jnp.int32) + (gb == n_blocks - 1).astype(jnp.int32)
        for p in range(2):
            kp = kw[pl.ds(row0, win), p * LANES:(p + 1) * LANES]
            vp = vw[pl.ds(row0, win), p * LANES:(p + 1) * LANES]
            res = []
            for e in range(2):
                qs = jnp.concatenate(
                    [jnp.where(keep[e],
                               q_ref[pl.ds(row0, blk), p * 512 + r * LANES:p * 512 + (r + 1) * LANES],
                               jnp.zeros((), BF16))
                     for r in range(4)], axis=0)
                s = _dot_nt(qs, kp) + bias_ref[kind, p, e * rows:(e + 1) * rows, :]
                m = jnp.max(s, axis=-1, keepdims=True)
                if has_sink:
                    sk = sink_ref[p, e * rows:(e + 1) * rows, :]
                    m = jnp.maximum(m, sk)
                pr = jnp.exp(s - m)
                l = jnp.sum(pr, axis=-1, keepdims=True)
                if has_sink:
                    l = l + jnp.exp(sk - m)
                acc = _dot(pr.astype(BF16), vp)
                res.append(acc * (1.0 / l))
                if has_lse:
                    lse = m + jnp.log(l)
                    for r in range(4):
                        hcol = p * 8 + e * 4 + r
                        lse_ref[pl.ds(row0, blk), hcol:hcol + 1] = lse[r * blk:(r + 1) * blk]
            for r in range(4):
                o = jnp.where(keep[0], res[0][r * blk:(r + 1) * blk], res[1][r * blk:(r + 1) * blk])
                o_ref[pl.ds(row0, blk), p * 512 + r * LANES:p * 512 + (r + 1) * LANES] = o.astype(BF16)
        return carry

    lax.fori_loop(0, n_inner, body, 0)


def _attn_call(q, k, v, bias, sink_rows, dil, hw, has_lse, tq):
    bn, ls, _ = q.shape
    tq = min(tq, ls)
    blk = ATTN_BLK
    assert ls % tq == 0 and tq % blk == 0 and tq % hw == 0 and blk % hw == 0
    n_tiles = ls // tq
    ratio = tq // hw
    last_halo = ls // hw - 1

    center = lambda b, r, i: (b, i, r)
    left = lambda b, r, i: (b, jnp.maximum(i * ratio - 1, 0), r)
    right = lambda b, r, i: (b, jnp.minimum((i + 1) * ratio, last_halo), r)
    kv_specs = [pl.BlockSpec((None, hw, KV_WIDTH), left),
                pl.BlockSpec((None, tq, KV_WIDTH), center),
                pl.BlockSpec((None, hw, KV_WIDTH), right)]
    in_specs = [pl.BlockSpec((None, tq, Q_WIDTH), center)] + kv_specs + kv_specs + [_resident(bias.shape)]
    args = [q, k, k, k, v, v, v, bias]
    if sink_rows is not None:
        in_specs.append(_resident(sink_rows.shape))
        args.append(sink_rows)
    out_shape = [jax.ShapeDtypeStruct(q.shape, BF16)]
    out_specs = [pl.BlockSpec((None, tq, Q_WIDTH), center)]
    if has_lse:
        out_shape.append(jax.ShapeDtypeStruct((bn, dil, ls, N_HEADS), F32))
        out_specs.append(pl.BlockSpec((None, None, tq, N_HEADS), lambda b, r, i: (b, r, i, 0)))
    win_rows = tq + 2 * hw
    return pl.pallas_call(
        functools.partial(_attn_kernel, tq=tq, hw=hw, n_blocks=ls // blk,
                          has_sink=sink_rows is not None, has_lse=has_lse),
        grid=(bn, dil, n_tiles),
        in_specs=in_specs,
        out_specs=out_specs,
        out_shape=out_shape,
        scratch_shapes=[pltpu.VMEM((win_rows, KV_WIDTH), BF16), pltpu.VMEM((win_rows, KV_WIDTH), BF16)],
        compiler_params=pltpu.CompilerParams(dimension_semantics=("parallel", "parallel", "parallel"),
                                             vmem_limit_bytes=VMEM_LIMIT),
        name=f"attn_d{dil}",
    )(*args)


FFN_CHUNKS = ((0, 1536), (1536, 1280))


def _post_kernel(*refs, n_groups):
    o_refs, refs = refs[:n_groups], refs[n_groups:]
    if n_groups > 1:
        lse_refs, refs = refs[:n_groups], refs[n_groups:]
        e_ref, refs = refs[0], refs[1:]
    x_ref, wo_ref, g_ref, wgu_ref, wd_ref, out_ref = refs

    if n_groups == 1:
        o = o_refs[0][...]
    else:
        lses = [r[...] for r in lse_refs]
        mx = functools.reduce(jnp.maximum, lses)
        ex = [jnp.exp(l - mx) for l in lses]
        inv = 1.0 / functools.reduce(jnp.add, ex)
        o = None
        for g in range(n_groups):
            wt = ex[g] * inv
            hi = wt.astype(BF16)
            lo = (wt - hi.astype(F32)).astype(BF16)
            wexp = _dot(jnp.concatenate([hi, lo], axis=-1), e_ref[...])
            term = wexp * o_refs[g][...].astype(F32)
            o = term if o is None else o + term
        o = o.astype(BF16)

    x1 = x_ref[...] + _dot(o, wo_ref[...])
    ms = jnp.mean(x1 * x1, axis=-1, keepdims=True)
    h = (x1 * lax.rsqrt(ms + EPS) * g_ref[...]).astype(BF16)
    acc = x1
    for c0, cw in FFN_CHUNKS:
        gate = _dot(h, wgu_ref[:, c0:c0 + cw])
        up = _dot(h, wgu_ref[:, D_FF + c0:D_FF + c0 + cw])
        act = (gate * jax.nn.sigmoid(gate) * up).astype(BF16)
        acc = acc + _dot(act, wd_ref[c0:c0 + cw, :])
    out_ref[...] = acc


def _post_call(os_, lses, expand, x2d, wo, gain, wgu, wd, tm=512):
    m = x2d.shape[0]
    n_groups = len(os_)
    row = lambda width: pl.BlockSpec((tm, width), lambda i: (i, 0))
    in_specs = [row(Q_WIDTH)] * n_groups
    args = list(os_)
    if n_groups > 1:
        in_specs += [row(N_HEADS)] * n_groups + [_resident(expand.shape)]
        args += list(lses) + [expand]
    in_specs += [row(D_MODEL), _resident(wo.shape), _resident(gain.shape), _resident(wgu.shape),
                 _resident(wd.shape)]
    args += [x2d, wo, gain, wgu, wd]
    return pl.pallas_call(
        functools.partial(_post_kernel, n_groups=n_groups),
        grid=(m // tm,),
        in_specs=in_specs,
        out_specs=row(D_MODEL),
        out_shape=jax.ShapeDtypeStruct((m, D_MODEL), F32),
        compiler_params=pltpu.CompilerParams(dimension_semantics=("parallel",),
                                             vmem_limit_bytes=VMEM_LIMIT),
        name=f"post{n_groups}",
    )(*args)


def _t5_buckets(rel):
    nb = NUM_BUCKETS // 2
    max_exact = nb // 2
    n = np.abs(rel)
    large = max_exact + (np.log(np.maximum(n, 1) / max_exact)
                         / math.log(MAX_DISTANCE / max_exact) * (nb - max_exact)).astype(np.int32)
    large = np.minimum(large, nb - 1)
    return ((rel > 0).astype(np.int32) * nb + np.where(n < max_exact, n, large)).astype(np.int32)


def _bias_table(rel_table, hw, dil):
    blk = ATTN_BLK
    win = blk + 2 * hw
    rel = np.arange(win)[None, :] - hw - np.arange(blk)[:, None]
    band = np.abs(rel) <= hw
    kj = np.arange(win)[None, :]
    masks = []
    for first in (False, True):
        for last in (False, True):
            ok = band.copy()
            if first:
                ok &= kj >= hw
            if last:
                ok &= kj < hw + blk
            masks.append(ok)
    mask = jnp.asarray(np.stack(masks))[:, None, None]
    tb = jnp.take(rel_table.astype(F32), jnp.asarray(_t5_buckets(dil * rel)), axis=0)
    tb = tb.transpose(2, 0, 1).reshape(1, 2, 8, blk, win)
    return jnp.where(mask, tb, NEG).reshape(4, 2, 8 * blk, win)


def _pair_cols(w):
    lead = w.shape[:-1]
    return jnp.swapaxes(w.reshape(*lead, 2, 2, 4, HEAD_DIM), -3, -2).reshape(*lead, Q_WIDTH)


def _qkv_weight(w, n_groups):
    parts = []
    for g in range(n_groups):
        base = g * QKV_WIDTH
        parts.append(_pair_cols(w[:, base:base + Q_WIDTH]))
        parts.append(w[:, base + Q_WIDTH:base + QKV_WIDTH])
    return jnp.concatenate(parts, axis=1).astype(BF16)


def _out_weight(w):
    return jnp.swapaxes(w.reshape(2, 2, 4, HEAD_DIM, D_MODEL), 1, 2).reshape(Q_WIDTH, D_MODEL).astype(BF16)


def _block_diag_ones():
    idx = np.arange(MXU_TILE) // HEAD_DIM
    return jnp.asarray((idx[:, None] == idx[None, :]).astype(np.float32), BF16)


def _expand_matrix():
    col = np.arange(Q_WIDTH)
    p, r, e = col // 512, (col % 512) // LANES, (col % LANES) // HEAD_DIM
    head = (2 * p + e) * 4 + r
    m = (np.arange(N_HEADS)[:, None] == head[None, :]).astype(np.float32)
    return jnp.asarray(np.concatenate([m, m], axis=0), BF16)


def _trunk(x, p):
    bn, s, _ = x.shape
    m = bn * s
    x2d = x.reshape(m, D_MODEL)

    q, k, v = _qkv_call(x2d, p["norm_attn"][0], p["a_w"], p["bd"], p["a_qg"], p["a_kg"], 1)
    (o,) = _attn_call(q.reshape(bn, s, Q_WIDTH), k.reshape(bn, s, KV_WIDTH), v.reshape(bn, s, KV_WIDTH),
                      p["a_bias"], p["a_sink"], 1, A_HALF_WINDOW, False, 512)
    x2d = _post_call([o.reshape(m, Q_WIDTH)], None, None, x2d, p["a_wo"], p["norm_ffn"][0],
                     p["wgu"][0], p["wd"][0])

    outs = _qkv_call(x2d, p["norm_attn"][1], p["b_w"], p["bd"], p["b_qg"], p["b_kg"], len(DILATED_GROUPS))
    os_, lses = [], []
    for gi, (window, dil) in enumerate(DILATED_GROUPS):
        q, k, v = outs[3 * gi:3 * gi + 3]
        ls = s // dil
        o, lse = _attn_call(q.reshape(bn, ls, dil * Q_WIDTH), k.reshape(bn, ls, dil * KV_WIDTH),
                            v.reshape(bn, ls, dil * KV_WIDTH), p["b_bias"][gi], None, dil,
                            window // (2 * dil), True, 512)
        os_.append(o.reshape(m, Q_WIDTH))
        lses.append(lse.transpose(0, 2, 1, 3).reshape(m, N_HEADS))
    x2d = _post_call(os_, lses, p["expand"], x2d, p["b_wo"], p["norm_ffn"][1], p["wgu"][1], p["wd"][1])
    return x2d.reshape(bn, s, D_MODEL)


def kernel(x_prompt, x_sample, rel_table, norm_attn, norm_ffn, a_w_qkv, a_q_gain, a_k_gain, a_sink, a_w_o,
           b_w_qkv, b_q_gain, b_k_gain, b_w_o, ffn_w_gate_up, ffn_w_down):
    n_grp = len(DILATED_GROUPS)
    scale = HEAD_DIM ** -0.5
    blk = ATTN_BLK
    sink_rows = jnp.broadcast_to(a_sink[0].astype(F32).reshape(2, 8, 1, 1), (2, 8, blk, 1)).reshape(2, 8 * blk, 1)
    p = {
        "norm_attn": norm_attn.astype(F32).reshape(-1, 1, D_MODEL),
        "norm_ffn": norm_ffn.astype(F32).reshape(-1, 1, D_MODEL),
        "bd": _block_diag_ones(),
        "expand": _expand_matrix(),
        "a_w": _qkv_weight(a_w_qkv[0], 1),
        "a_qg": (jnp.tile(a_q_gain[0].astype(F32), N_HEADS) * scale).reshape(1, 1, Q_WIDTH),
        "a_kg": jnp.tile(a_k_gain[0].astype(F32), N_KV_HEADS).reshape(1, 1, KV_WIDTH),
        "a_bias": _bias_table(rel_table, A_HALF_WINDOW, 1),
        "a_sink": sink_rows,
        "a_wo": _out_weight(a_w_o[0]),
        "b_w": _qkv_weight(b_w_qkv[0], n_grp),
        "b_qg": (jnp.tile(b_q_gain[0].astype(F32), (1, N_HEADS)) * scale).reshape(n_grp, 1, Q_WIDTH),
        "b_kg": jnp.tile(b_k_gain[0].astype(F32), (1, N_KV_HEADS)).reshape(n_grp, 1, KV_WIDTH),
        "b_bias": [_bias_table(rel_table, w // (2 * d), d) for w, d in DILATED_GROUPS],
        "b_wo": _out_weight(b_w_o[0]),
        "wgu": ffn_w_gate_up.astype(BF16),
        "wd": ffn_w_down.astype(BF16),
    }
    return _trunk(x_prompt, p), _trunk(x_sample, p)
```

```python
import functools
import math

import numpy as np
import jax
import jax.numpy as jnp
from jax import lax
from jax.experimental import pallas as pl
from jax.experimental.pallas import tpu as pltpu

D_MODEL = 1024
N_HEADS = 16
N_KV_HEADS = 4
HEAD_DIM = 64
Q_WIDTH = N_HEADS * HEAD_DIM
KV_WIDTH = N_KV_HEADS * HEAD_DIM
QKV_WIDTH = Q_WIDTH + 2 * KV_WIDTH
D_FF = 2816
A_HALF_WINDOW = 128
DILATED_GROUPS = ((128, 1), (512, 4), (2048, 16))
NUM_BUCKETS = 32
MAX_DISTANCE = 1024
EPS = 1e-6
NEG = -1e30

LANES = 128
MXU_TILE = 256
ATTN_BLK = 128
VMEM_LIMIT = 56 * 1024 * 1024

BF16 = jnp.bfloat16
F32 = jnp.float32


def _dot(a, b):
    return jnp.dot(a, b, preferred_element_type=F32)


def _dot_nt(a, b):
    return lax.dot_general(a, b, (((1,), (1,)), ((), ())), preferred_element_type=F32)


def _resident(shape):
    nd = len(shape)
    return pl.BlockSpec(shape, lambda *_: (0,) * nd, pipeline_mode=pl.Buffered(1))


def _qkv_kernel(x_ref, g_ref, w_ref, bd_ref, qg_ref, kg_ref, *refs, dils):
    out_refs, stage = refs[:-1], refs[-1]
    tm = x_ref.shape[0]
    x = x_ref[...]
    ms = jnp.mean(x * x, axis=-1, keepdims=True)
    h = (x * lax.rsqrt(ms + EPS) * g_ref[...]).astype(BF16)
    bd = bd_ref[...]

    def head_norm(t, gain):
        ss = _dot((t * t).astype(BF16), bd)
        return t * lax.rsqrt(ss * (1.0 / HEAD_DIM) + EPS) * gain

    def emit(ref, dil, width, lo, val):
        if dil == 1:
            ref[:, lo:lo + MXU_TILE] = val.astype(BF16)
            return
        for c in range(MXU_TILE // LANES):
            stage[c] = val[:, c * LANES:(c + 1) * LANES]
        for r in range(dil):
            for c in range(MXU_TILE // LANES):
                col = r * width + lo + c * LANES
                ref[:, col:col + LANES] = stage[c, pl.ds(r, tm // dil, stride=dil), :].astype(BF16)

    for g, dil in enumerate(dils):
        q_ref, k_ref, v_ref = out_refs[3 * g:3 * g + 3]
        base = g * QKV_WIDTH
        for c in range(Q_WIDTH // MXU_TILE):
            lo = c * MXU_TILE
            t = _dot(h, w_ref[:, base + lo:base + lo + MXU_TILE])
            emit(q_ref, dil, Q_WIDTH, lo, head_norm(t, qg_ref[g, :, lo:lo + MXU_TILE]))
        t = _dot(h, w_ref[:, base + Q_WIDTH:base + Q_WIDTH + KV_WIDTH])
        emit(k_ref, dil, KV_WIDTH, 0, head_norm(t, kg_ref[g]))
        emit(v_ref, dil, KV_WIDTH, 0, _dot(h, w_ref[:, base + Q_WIDTH + KV_WIDTH:base + QKV_WIDTH]))


def _qkv_call(x2d, gain, w, bd, qg, kg, dils, tm=512):
    m = x2d.shape[0]
    out_shape, out_specs = [], []
    for dil in dils:
        for width in (Q_WIDTH, KV_WIDTH, KV_WIDTH):
            out_shape.append(jax.ShapeDtypeStruct((m // dil, dil * width), BF16))
            out_specs.append(pl.BlockSpec((tm // dil, dil * width), lambda i: (i, 0)))
    return pl.pallas_call(
        functools.partial(_qkv_kernel, dils=tuple(dils)),
        grid=(m // tm,),
        in_specs=[pl.BlockSpec((tm, D_MODEL), lambda i: (i, 0)), _resident(gain.shape), _resident(w.shape),
                  _resident(bd.shape), _resident(qg.shape), _resident(kg.shape)],
        out_specs=out_specs,
        out_shape=out_shape,
        scratch_shapes=[pltpu.VMEM((MXU_TILE // LANES, tm, LANES), F32)],
        compiler_params=pltpu.CompilerParams(dimension_semantics=("parallel",),
                                             vmem_limit_bytes=VMEM_LIMIT),
        name=f"qkv{len(dils)}",
    )(x2d, gain, w, bd, qg, kg)


def _attn_kernel(*refs, tq, hw, n_blocks, has_sink, has_lse):
    q_ref, kl_ref, kc_ref, kr_ref, vl_ref, vc_ref, vr_ref, bias_ref = refs[:8]
    refs = refs[8:]
    sink_ref = None
    if has_sink:
        sink_ref, refs = refs[0], refs[1:]
    o_ref, refs = refs[0], refs[1:]
    lse_ref = None
    if has_lse:
        lse_ref, refs = refs[0], refs[1:]
    kw, vw = refs

    blk = ATTN_BLK
    win = blk + 2 * hw
    rows = 4 * blk

    for dst, (l, c, r) in ((kw, (kl_ref, kc_ref, kr_ref)), (vw, (vl_ref, vc_ref, vr_ref))):
        dst[0:hw, :] = l[...]
        dst[hw:hw + tq, :] = c[...]
        dst[hw + tq:hw + tq + hw, :] = r[...]

    tile = pl.program_id(2)
    n_inner = tq // blk
    lane = lax.broadcasted_iota(jnp.int32, (blk, LANES), 1)
    keep = (lane < HEAD_DIM, lane >= HEAD_DIM)

    def body(j, carry):
        row0 = pl.multiple_of(j * blk, blk)
        gb = tile * n_inner + j
        kind = 2 * (gb == 0).astype(jnp.int32) + (gb == n_blocks - 1).astype(jnp.int32)
        for p in range(2):
            kp = kw[pl.ds(row0, win), p * LANES:(p + 1) * LANES]
            vp = vw[pl.ds(row0, win), p * LANES:(p + 1) * LANES]
            res = []
            for e in range(2):
                qs = jnp.concatenate(
                    [jnp.where(keep[e],
                               q_ref[pl.ds(row0, blk), p * 512 + r * LANES:p * 512 + (r + 1) * LANES],
                               jnp.zeros((), BF16))
                     for r in range(4)], axis=0)
                s = _dot_nt(qs, kp) + bias_ref[kind, p, e * rows:(e + 1) * rows, :]
                tiles = [s[:, t * LANES:(t + 1) * LANES] for t in range(win // LANES)]
                mt = functools.reduce(jnp.maximum, tiles)
                if has_sink:
                    sk = sink_ref[p, e * rows:(e + 1) * rows, :]
                    mt = jnp.maximum(mt, sk)
                m = jnp.max(mt, axis=-1, keepdims=True)
                ps = [jnp.exp(t - m) for t in tiles]
                lt = functools.reduce(jnp.add, ps)
                if has_sink:
                    lt = lt + jnp.exp(sk - m)
                l = jnp.sum(lt, axis=-1, keepdims=True)
                pr = jnp.concatenate(ps, axis=1)
                acc = _dot(pr.astype(BF16), vp)
                res.append(acc * (1.0 / l))
                if has_lse:
                    lse = m + jnp.log(l)
                    for r in range(4):
                        hcol = p * 8 + e * 4 + r
                        lse_ref[pl.ds(row0, blk), hcol:hcol + 1] = lse[r * blk:(r + 1) * blk]
            for r in range(4):
                o = jnp.where(keep[0], res[0][r * blk:(r + 1) * blk], res[1][r * blk:(r + 1) * blk])
                o_ref[pl.ds(row0, blk), p * 512 + r * LANES:p * 512 + (r + 1) * LANES] = o.astype(BF16)
        return carry

    lax.fori_loop(0, n_inner, body, 0)


def _attn_call(q, k, v, bias, sink_rows, dil, hw, has_lse, tq):
    bn, ls, _ = q.shape
    tq = min(tq, ls)
    blk = ATTN_BLK
    assert ls % tq == 0 and tq % blk == 0 and tq % hw == 0 and blk % hw == 0
    n_tiles = ls // tq
    ratio = tq // hw
    last_halo = ls // hw - 1

    center = lambda b, r, i: (b, i, r)
    left = lambda b, r, i: (b, jnp.maximum(i * ratio - 1, 0), r)
    right = lambda b, r, i: (b, jnp.minimum((i + 1) * ratio, last_halo), r)
    kv_specs = [pl.BlockSpec((None, hw, KV_WIDTH), left),
                pl.BlockSpec((None, tq, KV_WIDTH), center),
                pl.BlockSpec((None, hw, KV_WIDTH), right)]
    in_specs = [pl.BlockSpec((None, tq, Q_WIDTH), center)] + kv_specs + kv_specs + [_resident(bias.shape)]
    args = [q, k, k, k, v, v, v, bias]
    if sink_rows is not None:
        in_specs.append(_resident(sink_rows.shape))
        args.append(sink_rows)
    out_shape = [jax.ShapeDtypeStruct(q.shape, BF16)]
    out_specs = [pl.BlockSpec((None, tq, Q_WIDTH), center)]
    if has_lse:
        out_shape.append(jax.ShapeDtypeStruct((bn, dil, ls, N_HEADS), F32))
        out_specs.append(pl.BlockSpec((None, None, tq, N_HEADS), lambda b, r, i: (b, r, i, 0)))
    win_rows = tq + 2 * hw
    return pl.pallas_call(
        functools.partial(_attn_kernel, tq=tq, hw=hw, n_blocks=ls // blk,
                          has_sink=sink_rows is not None, has_lse=has_lse),
        grid=(bn, dil, n_tiles),
        in_specs=in_specs,
        out_specs=out_specs,
        out_shape=out_shape,
        scratch_shapes=[pltpu.VMEM((win_rows, KV_WIDTH), BF16), pltpu.VMEM((win_rows, KV_WIDTH), BF16)],
        compiler_params=pltpu.CompilerParams(dimension_semantics=("parallel", "parallel", "parallel"),
                                             vmem_limit_bytes=VMEM_LIMIT),
        name=f"attn_d{dil}",
    )(*args)


FFN_CHUNKS = ((0, 1536), (1536, 1280))


def _post_kernel(*refs, dils):
    n_groups = len(dils)
    o_refs, refs = refs[:n_groups], refs[n_groups:]
    if n_groups > 1:
        lse_refs, refs = refs[:n_groups], refs[n_groups:]
        e_ref, refs = refs[0], refs[1:]
    x_ref, wo_ref, g_ref, wgu_ref, wd_ref, out_ref = refs[:6]
    stages = list(refs[6:])
    tm = x_ref.shape[0]

    def token_rows(g):
        dil = dils[g]
        if dil == 1:
            return o_refs[g][...].astype(F32)
        stage = stages.pop()
        n_planes = Q_WIDTH // LANES
        for r in range(dil):
            for c in range(n_planes):
                col = r * Q_WIDTH + c * LANES
                stage[c, pl.ds(r, tm // dil, stride=dil), :] = o_refs[g][:, col:col + LANES].astype(F32)
        return jnp.concatenate([stage[c] for c in range(n_planes)], axis=1)

    if n_groups == 1:
        o = o_refs[0][...]
    else:
        lses = [r[...] for r in lse_refs]
        mx = functools.reduce(jnp.maximum, lses)
        ex = [jnp.exp(l - mx) for l in lses]
        inv = 1.0 / functools.reduce(jnp.add, ex)
        o = None
        for g in range(n_groups):
            wt = ex[g] * inv
            hi = wt.astype(BF16)
            lo = (wt - hi.astype(F32)).astype(BF16)
            wexp = _dot(jnp.concatenate([hi, lo], axis=-1), e_ref[...])
            term = wexp * token_rows(g)
            o = term if o is None else o + term
        o = o.astype(BF16)

    x1 = x_ref[...] + _dot(o, wo_ref[...])
    ms = jnp.mean(x1 * x1, axis=-1, keepdims=True)
    h = (x1 * lax.rsqrt(ms + EPS) * g_ref[...]).astype(BF16)
    acc = x1
    for c0, cw in FFN_CHUNKS:
        gate = _dot(h, wgu_ref[:, c0:c0 + cw])
        up = _dot(h, wgu_ref[:, D_FF + c0:D_FF + c0 + cw])
        act = (gate * jax.nn.sigmoid(gate) * up).astype(BF16)
        acc = acc + _dot(act, wd_ref[c0:c0 + cw, :])
    out_ref[...] = acc


def _post_call(os_, dils, lses, expand, x2d, wo, gain, wgu, wd, tm=512):
    m = x2d.shape[0]
    n_groups = len(os_)
    row = lambda width: pl.BlockSpec((tm, width), lambda i: (i, 0))
    in_specs = [pl.BlockSpec((tm // dil, dil * Q_WIDTH), lambda i: (i, 0)) for dil in dils]
    args = list(os_)
    if n_groups > 1:
        in_specs += [row(N_HEADS)] * n_groups + [_resident(expand.shape)]
        args += list(lses) + [expand]
    in_specs += [row(D_MODEL), _resident(wo.shape), _resident(gain.shape), _resident(wgu.shape),
                 _resident(wd.shape)]
    args += [x2d, wo, gain, wgu, wd]
    return pl.pallas_call(
        functools.partial(_post_kernel, dils=tuple(dils)),
        grid=(m // tm,),
        in_specs=in_specs,
        out_specs=row(D_MODEL),
        out_shape=jax.ShapeDtypeStruct((m, D_MODEL), F32),
        scratch_shapes=[pltpu.VMEM((Q_WIDTH // LANES, tm, LANES), F32) for dil in dils if dil > 1],
        compiler_params=pltpu.CompilerParams(dimension_semantics=("parallel",),
                                             vmem_limit_bytes=VMEM_LIMIT),
        name=f"post{n_groups}",
    )(*args)


def _t5_buckets(rel):
    nb = NUM_BUCKETS // 2
    max_exact = nb // 2
    n = np.abs(rel)
    large = max_exact + (np.log(np.maximum(n, 1) / max_exact)
                         / math.log(MAX_DISTANCE / max_exact) * (nb - max_exact)).astype(np.int32)
    large = np.minimum(large, nb - 1)
    return ((rel > 0).astype(np.int32) * nb + np.where(n < max_exact, n, large)).astype(np.int32)


def _bias_table(rel_table, hw, dil):
    blk = ATTN_BLK
    win = blk + 2 * hw
    rel = np.arange(win)[None, :] - hw - np.arange(blk)[:, None]
    band = np.abs(rel) <= hw
    kj = np.arange(win)[None, :]
    masks = []
    for first in (False, True):
        for last in (False, True):
            ok = band.copy()
            if first:
                ok &= kj >= hw
            if last:
                ok &= kj < hw + blk
            masks.append(ok)
    mask = jnp.asarray(np.stack(masks))[:, None, None]
    n = win + blk - 1
    diff = np.arange(n)
    diff = np.where(diff < win, diff, diff - n) - hw
    w = jnp.take(rel_table.astype(F32), jnp.asarray(_t5_buckets(dil * diff)), axis=0).T
    tb = jnp.tile(w, (1, blk))[:, :blk * (n - 1)].reshape(N_HEADS, blk, n - 1)[:, :, :win]
    tb = tb.reshape(1, 2, 8, blk, win)
    return jnp.where(mask, tb, NEG).reshape(4, 2, 8 * blk, win)


def _pair_cols(w):
    lead = w.shape[:-1]
    return jnp.swapaxes(w.reshape(*lead, 2, 2, 4, HEAD_DIM), -3, -2).reshape(*lead, Q_WIDTH)


def _qkv_weight(w, n_groups):
    parts = []
    for g in range(n_groups):
        base = g * QKV_WIDTH
        parts.append(_pair_cols(w[:, base:base + Q_WIDTH]))
        parts.append(w[:, base + Q_WIDTH:base + QKV_WIDTH])
    return jnp.concatenate(parts, axis=1).astype(BF16)


def _out_weight(w):
    return jnp.swapaxes(w.reshape(2, 2, 4, HEAD_DIM, D_MODEL), 1, 2).reshape(Q_WIDTH, D_MODEL).astype(BF16)


def _block_diag_ones():
    idx = np.arange(MXU_TILE) // HEAD_DIM
    return jnp.asarray((idx[:, None] == idx[None, :]).astype(np.float32), BF16)


def _expand_matrix():
    col = np.arange(Q_WIDTH)
    p, r, e = col // 512, (col % 512) // LANES, (col % LANES) // HEAD_DIM
    head = (2 * p + e) * 4 + r
    m = (np.arange(N_HEADS)[:, None] == head[None, :]).astype(np.float32)
    return jnp.asarray(np.concatenate([m, m], axis=0), BF16)


def _trunk(x, p):
    bn, s, _ = x.shape
    m = bn * s
    x2d = x.reshape(m, D_MODEL)

    q, k, v = _qkv_call(x2d, p["norm_attn"][0], p["a_w"], p["bd"], p["a_qg"], p["a_kg"], (1,))
    (o,) = _attn_call(q.reshape(bn, s, Q_WIDTH), k.reshape(bn, s, KV_WIDTH), v.reshape(bn, s, KV_WIDTH),
                      p["a_bias"], p["a_sink"], 1, A_HALF_WINDOW, False, 512)
    x2d = _post_call([o.reshape(m, Q_WIDTH)], (1,), None, None, x2d, p["a_wo"], p["norm_ffn"][0],
                     p["wgu"][0], p["wd"][0])

    dils = tuple(d for _, d in DILATED_GROUPS)
    outs = _qkv_call(x2d, p["norm_attn"][1], p["b_w"], p["bd"], p["b_qg"], p["b_kg"], dils)
    os_, lses = [], []
    for gi, (window, dil) in enumerate(DILATED_GROUPS):
        q, k, v = outs[3 * gi:3 * gi + 3]
        ls = s // dil
        o, lse = _attn_call(q.reshape(bn, ls, dil * Q_WIDTH), k.reshape(bn, ls, dil * KV_WIDTH),
                            v.reshape(bn, ls, dil * KV_WIDTH), p["b_bias"][gi], None, dil,
                            window // (2 * dil), True, 512)
        os_.append(o.reshape(m // dil, dil * Q_WIDTH))
        lses.append(lse.transpose(0, 2, 1, 3).reshape(m, N_HEADS))
    x2d = _post_call(os_, dils, lses, p["expand"], x2d, p["b_wo"], p["norm_ffn"][1], p["wgu"][1], p["wd"][1])
    return x2d.reshape(bn, s, D_MODEL)


def kernel(x_prompt, x_sample, rel_table, norm_attn, norm_ffn, a_w_qkv, a_q_gain, a_k_gain, a_sink, a_w_o,
           b_w_qkv, b_q_gain, b_k_gain, b_w_o, ffn_w_gate_up, ffn_w_down):
    n_grp = len(DILATED_GROUPS)
    scale = HEAD_DIM ** -0.5
    blk = ATTN_BLK
    sink_rows = jnp.broadcast_to(a_sink[0].astype(F32).reshape(2, 8, 1, 1), (2, 8, blk, LANES))
    sink_rows = jnp.where(jnp.arange(LANES) == 0, sink_rows, NEG).reshape(2, 8 * blk, LANES)
    p = {
        "norm_attn": norm_attn.astype(F32).reshape(-1, 1, D_MODEL),
        "norm_ffn": norm_ffn.astype(F32).reshape(-1, 1, D_MODEL),
        "bd": _block_diag_ones(),
        "expand": _expand_matrix(),
        "a_w": _qkv_weight(a_w_qkv[0], 1),
        "a_qg": (jnp.tile(a_q_gain[0].astype(F32), N_HEADS) * scale).reshape(1, 1, Q_WIDTH),
        "a_kg": jnp.tile(a_k_gain[0].astype(F32), N_KV_HEADS).reshape(1, 1, KV_WIDTH),
        "a_bias": _bias_table(rel_table, A_HALF_WINDOW, 1),
        "a_sink": sink_rows,
        "a_wo": _out_weight(a_w_o[0]),
        "b_w": _qkv_weight(b_w_qkv[0], n_grp),
        "b_qg": (jnp.tile(b_q_gain[0].astype(F32), (1, N_HEADS)) * scale).reshape(n_grp, 1, Q_WIDTH),
        "b_kg": jnp.tile(b_k_gain[0].astype(F32), (1, N_KV_HEADS)).reshape(n_grp, 1, KV_WIDTH),
        "b_bias": [_bias_table(rel_table, w // (2 * d), d) for w, d in DILATED_GROUPS],
        "b_wo": _out_weight(b_w_o[0]),
        "wgu": ffn_w_gate_up.astype(BF16),
        "wd": ffn_w_down.astype(BF16),
    }
    return _trunk(x_prompt, p), _trunk(x_sample, p)
```

```python
import functools
import math

import numpy as np
import jax
import jax.numpy as jnp
from jax import lax
from jax.experimental import pallas as pl
from jax.experimental.pallas import tpu as pltpu

D_MODEL = 1024
N_HEADS = 16
N_KV_HEADS = 4
HEAD_DIM = 64
Q_WIDTH = N_HEADS * HEAD_DIM
KV_WIDTH = N_KV_HEADS * HEAD_DIM
QKV_WIDTH = Q_WIDTH + 2 * KV_WIDTH
D_FF = 2816
A_HALF_WINDOW = 128
DILATED_GROUPS = ((128, 1), (512, 4), (2048, 16))
NUM_BUCKETS = 32
MAX_DISTANCE = 1024
EPS = 1e-6
NEG = -1e30

LANES = 128
MXU_TILE = 256
ATTN_BLK = 128
VMEM_LIMIT = 56 * 1024 * 1024

BF16 = jnp.bfloat16
F32 = jnp.float32


def _dot(a, b):
    return jnp.dot(a, b, preferred_element_type=F32)


def _dot_nt(a, b):
    return lax.dot_general(a, b, (((1,), (1,)), ((), ())), preferred_element_type=F32)


def _resident(shape):
    nd = len(shape)
    return pl.BlockSpec(shape, lambda *_: (0,) * nd, pipeline_mode=pl.Buffered(1))


def _qkv_kernel(x_ref, g_ref, w_ref, bd_ref, qg_ref, kg_ref, *refs, dils):
    out_refs, stage = refs[:-1], refs[-1]
    tm = x_ref.shape[0]
    x = x_ref[...]
    ms = jnp.mean(x * x, axis=-1, keepdims=True)
    h = (x * lax.rsqrt(ms + EPS) * g_ref[...]).astype(BF16)
    bd = bd_ref[...]


    def emit(ref, dil, width, lo, val):
        if dil == 1:
            ref[:, lo:lo + MXU_TILE] = val.astype(BF16)
            return
        for c in range(MXU_TILE // LANES):
            stage[c] = val[:, c * LANES:(c + 1) * LANES]
        for r in range(dil):
            for c in range(MXU_TILE // LANES):
                col = r * width + lo + c * LANES
                ref[:, col:col + LANES] = stage[c, pl.ds(r, tm // dil, stride=dil), :].astype(BF16)

    n_norm = (Q_WIDTH + KV_WIDTH) // MXU_TILE
    for g, dil in enumerate(dils):
        q_ref, k_ref, v_ref = out_refs[3 * g:3 * g + 3]
        t = _dot(h, w_ref[:, g * QKV_WIDTH:(g + 1) * QKV_WIDTH])
        cols = [t[:, c * MXU_TILE:(c + 1) * MXU_TILE] for c in range(QKV_WIDTH // MXU_TILE)]
        sq = jnp.concatenate([(c * c).astype(BF16) for c in cols[:n_norm]], axis=0)
        ss = _dot(sq, bd)
        for c in range(n_norm):
            lo = c * MXU_TILE
            gain = qg_ref[g, :, lo:lo + MXU_TILE] if lo < Q_WIDTH else kg_ref[g]
            val = cols[c] * lax.rsqrt(ss[c * tm:(c + 1) * tm] * (1.0 / HEAD_DIM) + EPS) * gain
            if lo < Q_WIDTH:
                emit(q_ref, dil, Q_WIDTH, lo, val)
            else:
                emit(k_ref, dil, KV_WIDTH, 0, val)
        emit(v_ref, dil, KV_WIDTH, 0, cols[n_norm])


def _qkv_call(x2d, gain, w, bd, qg, kg, dils, tm=512):
    m = x2d.shape[0]
    out_shape, out_specs = [], []
    for dil in dils:
        for width in (Q_WIDTH, KV_WIDTH, KV_WIDTH):
            out_shape.append(jax.ShapeDtypeStruct((m // dil, dil * width), BF16))
            out_specs.append(pl.BlockSpec((tm // dil, dil * width), lambda i: (i, 0)))
    return pl.pallas_call(
        functools.partial(_qkv_kernel, dils=tuple(dils)),
        grid=(m // tm,),
        in_specs=[pl.BlockSpec((tm, D_MODEL), lambda i: (i, 0)), _resident(gain.shape), _resident(w.shape),
                  _resident(bd.shape), _resident(qg.shape), _resident(kg.shape)],
        out_specs=out_specs,
        out_shape=out_shape,
        scratch_shapes=[pltpu.VMEM((MXU_TILE // LANES, tm, LANES), F32)],
        compiler_params=pltpu.CompilerParams(dimension_semantics=("parallel",),
                                             vmem_limit_bytes=VMEM_LIMIT),
        name=f"qkv{len(dils)}",
    )(x2d, gain, w, bd, qg, kg)


def _attn_kernel(*refs, tq, hw, n_blocks, has_sink, has_lse):
    q_ref, kl_ref, kc_ref, kr_ref, vl_ref, vc_ref, vr_ref, bias_ref = refs[:8]
    refs = refs[8:]
    sink_ref = None
    if has_sink:
        sink_ref, refs = refs[0], refs[1:]
    o_ref, refs = refs[0], refs[1:]
    lse_ref = None
    if has_lse:
        lse_ref, refs = refs[0], refs[1:]
    kw, vw = refs

    blk = ATTN_BLK
    win = blk + 2 * hw
    rows = 4 * blk

    for dst, (l, c, r) in ((kw, (kl_ref, kc_ref, kr_ref)), (vw, (vl_ref, vc_ref, vr_ref))):
        dst[0:hw, :] = l[...]
        dst[hw:hw + tq, :] = c[...]
        dst[hw + tq:hw + tq + hw, :] = r[...]

    tile = pl.program_id(2)
    n_inner = tq // blk
    lane = lax.broadcasted_iota(jnp.int32, (1, LANES), 1)
    keep = (lane < HEAD_DIM, lane >= HEAD_DIM)

    def body(j, carry):
        row0 = pl.multiple_of(j * blk, blk)
        gb = tile * n_inner + j
        kind = 2 * (gb == 0).astype(jnp.int32) + (gb == n_blocks - 1).astype(jnp.int32)
        lse_all = jnp.zeros((blk, LANES), F32)
        for p in range(2):
            kp = kw[pl.ds(row0, win), p * LANES:(p + 1) * LANES]
            vp = vw[pl.ds(row0, win), p * LANES:(p + 1) * LANES]
            accs, ms, sinks = [], [], []
            for e in range(2):
                qs = jnp.concatenate(
                    [jnp.where(keep[e],
                               q_ref[pl.ds(row0, blk), p * 512 + r * LANES:p * 512 + (r + 1) * LANES],
                               jnp.zeros((), BF16))
                     for r in range(4)], axis=0)
                s = _dot_nt(qs, kp) + bias_ref[kind, p, e * rows:(e + 1) * rows, :]
                mt = functools.reduce(jnp.maximum, [s[:, t * LANES:(t + 1) * LANES] for t in range(win // LANES)])
                if has_sink:
                    sk = sink_ref[p, e * rows:(e + 1) * rows, :]
                    mt = jnp.maximum(mt, sk)
                m = jnp.max(mt, axis=-1, keepdims=True)
                pr = jnp.exp((s - m).astype(BF16))
                acc = _dot(pr, jnp.where(keep[e], vp, jnp.ones((), BF16)))
                accs.append(acc)
                ms.append(m)
                if has_sink:
                    sinks.append(jnp.exp(sk - m))
            for r in range(4):
                sl = slice(r * blk, (r + 1) * blk)
                num = jnp.where(keep[0], accs[0][sl], accs[1][sl])
                den = pltpu.roll(jnp.where(keep[0], accs[1][sl], accs[0][sl]), HEAD_DIM, axis=1)
                if has_sink:
                    den = den + jnp.where(keep[0], sinks[0][sl], sinks[1][sl])
                o = num * (1.0 / den)
                o_ref[pl.ds(row0, blk), p * 512 + r * LANES:p * 512 + (r + 1) * LANES] = o.astype(BF16)
                if has_lse:
                    lse = jnp.where(keep[0], ms[0][sl], ms[1][sl]) + jnp.log(den)
                    c0 = p * 4 + r
                    lse_all = jnp.where((lane == c0) | (lane == HEAD_DIM + c0), lse, lse_all)
        if has_lse:
            lse_ref[pl.ds(row0, blk), :] = lse_all
        return carry

    lax.fori_loop(0, n_inner, body, 0)


def _attn_call(q, k, v, bias, sink_rows, dil, hw, has_lse, tq):
    bn, ls, _ = q.shape
    tq = min(tq, ls)
    blk = ATTN_BLK
    assert ls % tq == 0 and tq % blk == 0 and tq % hw == 0 and blk % hw == 0
    n_tiles = ls // tq
    ratio = tq // hw
    last_halo = ls // hw - 1

    center = lambda b, r, i: (b, i, r)
    left = lambda b, r, i: (b, jnp.maximum(i * ratio - 1, 0), r)
    right = lambda b, r, i: (b, jnp.minimum((i + 1) * ratio, last_halo), r)
    kv_specs = [pl.BlockSpec((None, hw, KV_WIDTH), left),
                pl.BlockSpec((None, tq, KV_WIDTH), center),
                pl.BlockSpec((None, hw, KV_WIDTH), right)]
    in_specs = [pl.BlockSpec((None, tq, Q_WIDTH), center)] + kv_specs + kv_specs + [_resident(bias.shape)]
    args = [q, k, k, k, v, v, v, bias]
    if sink_rows is not None:
        in_specs.append(_resident(sink_rows.shape))
        args.append(sink_rows)
    out_shape = [jax.ShapeDtypeStruct(q.shape, BF16)]
    out_specs = [pl.BlockSpec((None, tq, Q_WIDTH), center)]
    if has_lse:
        out_shape.append(jax.ShapeDtypeStruct((bn, ls, dil * LANES), F32))
        out_specs.append(pl.BlockSpec((None, tq, LANES), center))
    win_rows = tq + 2 * hw
    return pl.pallas_call(
        functools.partial(_attn_kernel, tq=tq, hw=hw, n_blocks=ls // blk,
                          has_sink=sink_rows is not None, has_lse=has_lse),
        grid=(bn, dil, n_tiles),
        in_specs=in_specs,
        out_specs=out_specs,
        out_shape=out_shape,
        scratch_shapes=[pltpu.VMEM((win_rows, KV_WIDTH), BF16), pltpu.VMEM((win_rows, KV_WIDTH), BF16)],
        compiler_params=pltpu.CompilerParams(dimension_semantics=("parallel", "parallel", "parallel"),
                                             vmem_limit_bytes=VMEM_LIMIT),
        name=f"attn_d{dil}",
    )(*args)


FFN_CHUNKS = ((0, 1536), (1536, 1280))


def _post_kernel(*refs, dils):
    n_groups = len(dils)
    o_refs, refs = refs[:n_groups], refs[n_groups:]
    if n_groups > 1:
        lse_refs, refs = refs[:n_groups], refs[n_groups:]
        e_ref, refs = refs[0], refs[1:]
    x_ref, wo_ref, g_ref, wgu_ref, wd_ref, out_ref = refs[:6]
    stages = dict(zip([g for g, dil in enumerate(dils) if dil > 1], refs[6:]))
    tm = x_ref.shape[0]

    def token_rows(ref, g, width, plane0):
        dil = dils[g]
        if dil == 1:
            return ref[...].astype(F32)
        stage = stages[g]
        n_planes = width // LANES
        for r in range(dil):
            for c in range(n_planes):
                col = r * width + c * LANES
                stage[plane0 + c, pl.ds(r, tm // dil, stride=dil), :] = ref[:, col:col + LANES].astype(F32)
        return jnp.concatenate([stage[plane0 + c] for c in range(n_planes)], axis=1)

    if n_groups == 1:
        o = o_refs[0][...]
    else:
        lses = [token_rows(lse_refs[g], g, LANES, Q_WIDTH // LANES) for g in range(n_groups)]
        mx = functools.reduce(jnp.maximum, lses)
        ex = [jnp.exp(l - mx) for l in lses]
        inv = 1.0 / functools.reduce(jnp.add, ex)
        o = None
        for g in range(n_groups):
            wt = ex[g] * inv
            hi = wt.astype(BF16)
            lo = (wt - hi.astype(F32)).astype(BF16)
            wexp = _dot(jnp.concatenate([hi, lo], axis=-1), e_ref[...])
            term = wexp * token_rows(o_refs[g], g, Q_WIDTH, 0)
            o = term if o is None else o + term
        o = o.astype(BF16)

    x1 = x_ref[...] + _dot(o, wo_ref[...])
    ms = jnp.mean(x1 * x1, axis=-1, keepdims=True)
    h = (x1 * lax.rsqrt(ms + EPS) * g_ref[...]).astype(BF16)
    acc = x1
    for c0, cw in FFN_CHUNKS:
        gate = _dot(h, wgu_ref[:, c0:c0 + cw])
        up = _dot(h, wgu_ref[:, D_FF + c0:D_FF + c0 + cw])
        act = (gate * jax.nn.sigmoid(gate) * up).astype(BF16)
        acc = acc + _dot(act, wd_ref[c0:c0 + cw, :])
    out_ref[...] = acc


def _post_call(os_, dils, lses, expand, x2d, wo, gain, wgu, wd, tm=512):
    m = x2d.shape[0]
    n_groups = len(os_)
    row = lambda width: pl.BlockSpec((tm, width), lambda i: (i, 0))
    in_specs = [pl.BlockSpec((tm // dil, dil * Q_WIDTH), lambda i: (i, 0)) for dil in dils]
    args = list(os_)
    if n_groups > 1:
        in_specs += [pl.BlockSpec((tm // dil, dil * LANES), lambda i: (i, 0)) for dil in dils]
        in_specs += [_resident(expand.shape)]
        args += list(lses) + [expand]
    in_specs += [row(D_MODEL), _resident(wo.shape), _resident(gain.shape), _resident(wgu.shape),
                 _resident(wd.shape)]
    args += [x2d, wo, gain, wgu, wd]
    return pl.pallas_call(
        functools.partial(_post_kernel, dils=tuple(dils)),
        grid=(m // tm,),
        in_specs=in_specs,
        out_specs=row(D_MODEL),
        out_shape=jax.ShapeDtypeStruct((m, D_MODEL), F32),
        scratch_shapes=[pltpu.VMEM((Q_WIDTH // LANES + 1, tm, LANES), F32) for dil in dils if dil > 1],
        compiler_params=pltpu.CompilerParams(dimension_semantics=("parallel",),
                                             vmem_limit_bytes=VMEM_LIMIT),
        name=f"post{n_groups}",
    )(*args)


def _t5_buckets(rel):
    nb = NUM_BUCKETS // 2
    max_exact = nb // 2
    n = np.abs(rel)
    large = max_exact + (np.log(np.maximum(n, 1) / max_exact)
                         / math.log(MAX_DISTANCE / max_exact) * (nb - max_exact)).astype(np.int32)
    large = np.minimum(large, nb - 1)
    return ((rel > 0).astype(np.int32) * nb + np.where(n < max_exact, n, large)).astype(np.int32)


def _bias_table(rel_table, hw, dil):
    blk = ATTN_BLK
    win = blk + 2 * hw
    rel = np.arange(win)[None, :] - hw - np.arange(blk)[:, None]
    band = np.abs(rel) <= hw
    kj = np.arange(win)[None, :]
    masks = []
    for first in (False, True):
        for last in (False, True):
            ok = band.copy()
            if first:
                ok &= kj >= hw
            if last:
                ok &= kj < hw + blk
            masks.append(ok)
    mask = jnp.asarray(np.stack(masks))[:, None, None]
    n = win + blk - 1
    diff = np.arange(n)
    diff = np.where(diff < win, diff, diff - n) - hw
    w = jnp.take(rel_table.astype(F32), jnp.asarray(_t5_buckets(dil * diff)), axis=0).T
    tb = jnp.tile(w, (1, blk))[:, :blk * (n - 1)].reshape(N_HEADS, blk, n - 1)[:, :, :win]
    tb = tb.reshape(1, 2, 8, blk, win)
    return jnp.where(mask, tb, NEG).reshape(4, 2, 8 * blk, win)


def _pair_cols(w):
    lead = w.shape[:-1]
    return jnp.swapaxes(w.reshape(*lead, 2, 2, 4, HEAD_DIM), -3, -2).reshape(*lead, Q_WIDTH)


def _qkv_weight(w, n_groups):
    parts = []
    for g in range(n_groups):
        base = g * QKV_WIDTH
        parts.append(_pair_cols(w[:, base:base + Q_WIDTH]))
        parts.append(w[:, base + Q_WIDTH:base + QKV_WIDTH])
    return jnp.concatenate(parts, axis=1).astype(BF16)


def _out_weight(w):
    return jnp.swapaxes(w.reshape(2, 2, 4, HEAD_DIM, D_MODEL), 1, 2).reshape(Q_WIDTH, D_MODEL).astype(BF16)


def _block_diag_ones():
    idx = np.arange(MXU_TILE) // HEAD_DIM
    return jnp.asarray((idx[:, None] == idx[None, :]).astype(np.float32), BF16)


def _expand_matrix():
    col = np.arange(Q_WIDTH)
    p, r, e = col // 512, (col % 512) // LANES, (col % LANES) // HEAD_DIM
    src = e * HEAD_DIM + p * 4 + r
    m = (np.arange(LANES)[:, None] == src[None, :]).astype(np.float32)
    return jnp.asarray(np.concatenate([m, m], axis=0), BF16)


def _trunk(x, p):
    bn, s, _ = x.shape
    m = bn * s
    x2d = x.reshape(m, D_MODEL)

    q, k, v = _qkv_call(x2d, p["norm_attn"][0], p["a_w"], p["bd"], p["a_qg"], p["a_kg"], (1,))
    (o,) = _attn_call(q.reshape(bn, s, Q_WIDTH), k.reshape(bn, s, KV_WIDTH), v.reshape(bn, s, KV_WIDTH),
                      p["a_bias"], p["a_sink"], 1, A_HALF_WINDOW, False, 512)
    x2d = _post_call([o.reshape(m, Q_WIDTH)], (1,), None, None, x2d, p["a_wo"], p["norm_ffn"][0],
                     p["wgu"][0], p["wd"][0])

    dils = tuple(d for _, d in DILATED_GROUPS)
    outs = _qkv_call(x2d, p["norm_attn"][1], p["b_w"], p["bd"], p["b_qg"], p["b_kg"], dils)
    os_, lses = [], []
    for gi, (window, dil) in enumerate(DILATED_GROUPS):
        q, k, v = outs[3 * gi:3 * gi + 3]
        ls = s // dil
        o, lse = _attn_call(q.reshape(bn, ls, dil * Q_WIDTH), k.reshape(bn, ls, dil * KV_WIDTH),
                            v.reshape(bn, ls, dil * KV_WIDTH), p["b_bias"][gi], None, dil,
                            window // (2 * dil), True, 512)
        os_.append(o.reshape(m // dil, dil * Q_WIDTH))
        lses.append(lse.reshape(m // dil, dil * LANES))
    x2d = _post_call(os_, dils, lses, p["expand"], x2d, p["b_wo"], p["norm_ffn"][1], p["wgu"][1], p["wd"][1])
    return x2d.reshape(bn, s, D_MODEL)


def kernel(x_prompt, x_sample, rel_table, norm_attn, norm_ffn, a_w_qkv, a_q_gain, a_k_gain, a_sink, a_w_o,
           b_w_qkv, b_q_gain, b_k_gain, b_w_o, ffn_w_gate_up, ffn_w_down):
    n_grp = len(DILATED_GROUPS)
    scale = HEAD_DIM ** -0.5
    blk = ATTN_BLK
    sink_rows = jnp.broadcast_to(a_sink[0].astype(F32).reshape(2, 8, 1, 1),
                                 (2, 8, blk, LANES)).reshape(2, 8 * blk, LANES)
    p = {
        "norm_attn": norm_attn.astype(F32).reshape(-1, 1, D_MODEL),
        "norm_ffn": norm_ffn.astype(F32).reshape(-1, 1, D_MODEL),
        "bd": _block_diag_ones(),
        "expand": _expand_matrix(),
        "a_w": _qkv_weight(a_w_qkv[0], 1),
        "a_qg": (jnp.tile(a_q_gain[0].astype(F32), N_HEADS) * scale).reshape(1, 1, Q_WIDTH),
        "a_kg": jnp.tile(a_k_gain[0].astype(F32), N_KV_HEADS).reshape(1, 1, KV_WIDTH),
        "a_bias": _bias_table(rel_table, A_HALF_WINDOW, 1),
        "a_sink": sink_rows,
        "a_wo": _out_weight(a_w_o[0]),
        "b_w": _qkv_weight(b_w_qkv[0], n_grp),
        "b_qg": (jnp.tile(b_q_gain[0].astype(F32), (1, N_HEADS)) * scale).reshape(n_grp, 1, Q_WIDTH),
        "b_kg": jnp.tile(b_k_gain[0].astype(F32), (1, N_KV_HEADS)).reshape(n_grp, 1, KV_WIDTH),
        "b_bias": [_bias_table(rel_table, w // (2 * d), d) for w, d in DILATED_GROUPS],
        "b_wo": _out_weight(b_w_o[0]),
        "wgu": ffn_w_gate_up.astype(BF16),
        "wd": ffn_w_down.astype(BF16),
    }
    return _trunk(x_prompt, p), _trunk(x_sample, p)
```

```python
import functools
import math

import numpy as np
import jax
import jax.numpy as jnp
from jax import lax
from jax.experimental import pallas as pl
from jax.experimental.pallas import tpu as pltpu

D_MODEL = 1024
N_HEADS = 16
N_KV_HEADS = 4
HEAD_DIM = 64
Q_WIDTH = N_HEADS * HEAD_DIM
KV_WIDTH = N_KV_HEADS * HEAD_DIM
QKV_WIDTH = Q_WIDTH + 2 * KV_WIDTH
D_FF = 2816
A_HALF_WINDOW = 128
DILATED_GROUPS = ((128, 1), (512, 4), (2048, 16))
NUM_BUCKETS = 32
MAX_DISTANCE = 1024
EPS = 1e-6
NEG = -1e30

LANES = 128
MXU_TILE = 256
ATTN_BLK = 128
ATTN_STEP_TOKENS = 512
VMEM_LIMIT = 56 * 1024 * 1024

BF16 = jnp.bfloat16
F32 = jnp.float32


def _dot(a, b):
    return jnp.dot(a, b, preferred_element_type=F32)


def _dot_nt(a, b):
    return lax.dot_general(a, b, (((1,), (1,)), ((), ())), preferred_element_type=F32)


def _dot_tn(a, b):
    return lax.dot_general(a, b, (((0,), (0,)), ((), ())), preferred_element_type=F32)


def _resident(shape):
    nd = len(shape)
    return pl.BlockSpec(shape, lambda *_: (0,) * nd, pipeline_mode=pl.Buffered(1))


def _qkv_kernel(x_ref, g_ref, w_ref, bd_ref, qg_ref, kg_ref, *refs, dils):
    out_refs, stage = refs[:-1], refs[-1]
    tm = x_ref.shape[0]
    x = x_ref[...]
    ms = jnp.mean(x * x, axis=-1, keepdims=True)
    h = (x * lax.rsqrt(ms + EPS) * g_ref[...]).astype(BF16)
    bd = bd_ref[...]


    def emit(ref, dil, width, lo, val):
        if dil == 1:
            ref[:, lo:lo + MXU_TILE] = val.astype(BF16)
            return
        for c in range(MXU_TILE // LANES):
            stage[c] = val[:, c * LANES:(c + 1) * LANES]
        for r in range(dil):
            for c in range(MXU_TILE // LANES):
                col = r * width + lo + c * LANES
                ref[:, col:col + LANES] = stage[c, pl.ds(r, tm // dil, stride=dil), :].astype(BF16)

    n_norm = (Q_WIDTH + KV_WIDTH) // MXU_TILE
    for g, dil in enumerate(dils):
        q_ref, k_ref, v_ref = out_refs[3 * g:3 * g + 3]
        t = _dot(h, w_ref[:, g * QKV_WIDTH:(g + 1) * QKV_WIDTH])
        cols = [t[:, c * MXU_TILE:(c + 1) * MXU_TILE] for c in range(QKV_WIDTH // MXU_TILE)]
        sq = jnp.concatenate([(c * c).astype(BF16) for c in cols[:n_norm]], axis=0)
        ss = _dot(sq, bd)
        for c in range(n_norm):
            lo = c * MXU_TILE
            gain = qg_ref[g, :, lo:lo + MXU_TILE] if lo < Q_WIDTH else kg_ref[g]
            val = cols[c] * lax.rsqrt(ss[c * tm:(c + 1) * tm] * (1.0 / HEAD_DIM) + EPS) * gain
            if lo < Q_WIDTH:
                emit(q_ref, dil, Q_WIDTH, lo, val)
            else:
                emit(k_ref, dil, KV_WIDTH, 0, val)
        emit(v_ref, dil, KV_WIDTH, 0, cols[n_norm])


def _qkv_call(x2d, gain, w, bd, qg, kg, dils, tm=512):
    m = x2d.shape[0]
    out_shape, out_specs = [], []
    for dil in dils:
        for width in (Q_WIDTH, KV_WIDTH, KV_WIDTH):
            out_shape.append(jax.ShapeDtypeStruct((m // dil, dil * width), BF16))
            out_specs.append(pl.BlockSpec((tm // dil, dil * width), lambda i: (i, 0)))
    return pl.pallas_call(
        functools.partial(_qkv_kernel, dils=tuple(dils)),
        grid=(m // tm,),
        in_specs=[pl.BlockSpec((tm, D_MODEL), lambda i: (i, 0)), _resident(gain.shape), _resident(w.shape),
                  _resident(bd.shape), _resident(qg.shape), _resident(kg.shape)],
        out_specs=out_specs,
        out_shape=out_shape,
        scratch_shapes=[pltpu.VMEM((MXU_TILE // LANES, tm, LANES), F32)],
        compiler_params=pltpu.CompilerParams(dimension_semantics=("parallel",),
                                             vmem_limit_bytes=VMEM_LIMIT),
        name=f"qkv{len(dils)}",
    )(x2d, gain, w, bd, qg, kg)


def _attn_kernel(*refs, tq, hw, n_res, n_blocks, has_sink, has_lse):
    q_ref, kl_ref, kc_ref, kr_ref, vl_ref, vc_ref, vr_ref, bias_ref = refs[:8]
    refs = refs[8:]
    sink_ref = None
    if has_sink:
        sink_ref, refs = refs[0], refs[1:]
    o_ref, refs = refs[0], refs[1:]
    lse_ref = None
    if has_lse:
        lse_ref, refs = refs[0], refs[1:]
    kw, vw = refs

    blk = ATTN_BLK
    win = blk + 2 * hw

    for dst, (l, c, r) in ((kw, (kl_ref, kc_ref, kr_ref)), (vw, (vl_ref, vc_ref, vr_ref))):
        dst[0:hw, :] = l[...]
        dst[hw:hw + tq, :] = c[...]
        dst[hw + tq:hw + tq + hw, :] = r[...]

    tile = pl.program_id(2)
    n_inner = tq // blk
    lane = lax.broadcasted_iota(jnp.int32, (1, LANES), 1)
    keep = (lane < HEAD_DIM, lane >= HEAD_DIM)
    first_half = lax.broadcasted_iota(jnp.int32, (LANES, 1), 0) < HEAD_DIM
    sub = lax.broadcasted_iota(jnp.int32, (8, 1), 0)

    chunks = [(j, u, p, e) for u in range(n_res) for j in range(n_inner) for p in range(2) for e in range(2)]
    kinds, lse_rows = {}, {}
    for j in range(n_inner):
        gb = tile * n_inner + j
        kinds[j] = 2 * (gb == 0).astype(jnp.int32) + (gb == n_blocks - 1).astype(jnp.int32)
    scores_, probs_, results_ = {}, {}, {}

    def scores(c):
        j, u, p, e = c
        kp = kw[j * blk:j * blk + win, u * KV_WIDTH + p * LANES:u * KV_WIDTH + (p + 1) * LANES]
        base = u * Q_WIDTH + p * 512
        qcat = jnp.concatenate([q_ref[j * blk:(j + 1) * blk, base + r * LANES:base + (r + 1) * LANES]
                                for r in range(4)], axis=0)
        k_e = jnp.where(keep[e], kp, jnp.zeros((), BF16))
        scores_[c] = _dot_nt(k_e, qcat) + bias_ref[kinds[j], p, e]

    def probs(c):
        j, u, p, e = c
        st = scores_.pop(c)
        m = jnp.max(st, axis=0, keepdims=True)
        sk = None
        if has_sink:
            sk = sink_ref[p, e]
            m = jnp.maximum(m, sk)
        probs_[c] = (jnp.exp((st - m).astype(BF16)), m, sk)

    def values(c):
        j, u, p, e = c
        pt, m, sk = probs_.pop(c)
        vp = vw[j * blk:j * blk + win, u * KV_WIDTH + p * LANES:u * KV_WIDTH + (p + 1) * LANES]
        v_e = jnp.where(keep[e], vp, jnp.ones((), BF16))
        ot = _dot_tn(v_e, pt)
        den = ot[HEAD_DIM:HEAD_DIM + 1] if e == 0 else ot[0:1]
        if has_sink:
            den = den + jnp.exp(sk - m)
        results_[c] = (ot, 1.0 / den, (m + jnp.log(den)) if has_lse else None)
        if e == 1:
            finish(j, u, p)

    def finish(j, u, p):
        (o0, i0, l0), (o1, i1, l1) = results_.pop((j, u, p, 0)), results_.pop((j, u, p, 1))
        zeros = jnp.zeros((8, blk), F32)
        lse_lo, lse_hi = lse_rows.get((j, u), (zeros, zeros))
        for r in range(4):
            cs = slice(r * blk, (r + 1) * blk)
            t = jnp.where(first_half, o0[:, cs] * i0[:, cs], o1[:, cs] * i1[:, cs])
            col = u * Q_WIDTH + p * 512 + r * LANES
            o_ref[j * blk:(j + 1) * blk, col:col + LANES] = t.T.astype(BF16)
            if has_lse:
                lse_lo = jnp.where(sub == p * 4 + r, l0[:, cs], lse_lo)
                lse_hi = jnp.where(sub == p * 4 + r, l1[:, cs], lse_hi)
        lse_rows[(j, u)] = (lse_lo, lse_hi)
        if has_lse and p == 1:
            z = jnp.zeros((HEAD_DIM - 8, blk), F32)
            lse_ref[j * blk:(j + 1) * blk, u * LANES:(u + 1) * LANES] = (
                jnp.concatenate([lse_lo, z, lse_hi, z], axis=0).T)

    n = len(chunks)
    for t in range(n + 2):
        if t < n:
            scores(chunks[t])
        if 0 <= t - 1 < n:
            probs(chunks[t - 1])
        if 0 <= t - 2 < n:
            values(chunks[t - 2])


def _attn_call(q, k, v, bias, sink_rows, dil, hw, has_lse):
    bn, ls, _ = q.shape
    blk = ATTN_BLK
    tq = min(ATTN_STEP_TOKENS, ls)
    n_res = ATTN_STEP_TOKENS // tq
    assert ls % tq == 0 and tq % blk == 0 and tq % hw == 0 and blk % hw == 0 and dil % n_res == 0
    n_tiles = ls // tq
    ratio = tq // hw
    last_halo = ls // hw - 1

    center = lambda b, r, i: (b, i, r)
    left = lambda b, r, i: (b, jnp.maximum(i * ratio - 1, 0), r)
    right = lambda b, r, i: (b, jnp.minimum((i + 1) * ratio, last_halo), r)
    kv_specs = [pl.BlockSpec((None, hw, n_res * KV_WIDTH), left),
                pl.BlockSpec((None, tq, n_res * KV_WIDTH), center),
                pl.BlockSpec((None, hw, n_res * KV_WIDTH), right)]
    in_specs = ([pl.BlockSpec((None, tq, n_res * Q_WIDTH), center)] + kv_specs + kv_specs
                + [_resident(bias.shape)])
    args = [q, k, k, k, v, v, v, bias]
    if sink_rows is not None:
        in_specs.append(_resident(sink_rows.shape))
        args.append(sink_rows)
    out_shape = [jax.ShapeDtypeStruct(q.shape, BF16)]
    out_specs = [pl.BlockSpec((None, tq, n_res * Q_WIDTH), center)]
    if has_lse:
        out_shape.append(jax.ShapeDtypeStruct((bn, ls, dil * LANES), F32))
        out_specs.append(pl.BlockSpec((None, tq, n_res * LANES), center))
    win_rows = tq + 2 * hw
    return pl.pallas_call(
        functools.partial(_attn_kernel, tq=tq, hw=hw, n_res=n_res, n_blocks=ls // blk,
                          has_sink=sink_rows is not None, has_lse=has_lse),
        grid=(bn, dil // n_res, n_tiles),
        in_specs=in_specs,
        out_specs=out_specs,
        out_shape=out_shape,
        scratch_shapes=[pltpu.VMEM((win_rows, n_res * KV_WIDTH), BF16),
                        pltpu.VMEM((win_rows, n_res * KV_WIDTH), BF16)],
        compiler_params=pltpu.CompilerParams(dimension_semantics=("parallel", "parallel", "parallel"),
                                             vmem_limit_bytes=VMEM_LIMIT),
        name=f"attn_d{dil}",
    )(*args)


FFN_CHUNKS = ((0, 1536), (1536, 1280))


def _post_kernel(*refs, dils):
    n_groups = len(dils)
    o_refs, refs = refs[:n_groups], refs[n_groups:]
    if n_groups > 1:
        lse_refs, refs = refs[:n_groups], refs[n_groups:]
        e_ref, refs = refs[0], refs[1:]
    x_ref, wo_ref, g_ref, wgu_ref, wd_ref, out_ref = refs[:6]
    stages = dict(zip([g for g, dil in enumerate(dils) if dil > 1], refs[6:]))
    tm = x_ref.shape[0]

    def token_rows(ref, g, width, plane0):
        dil = dils[g]
        if dil == 1:
            return ref[...].astype(F32)
        stage = stages[g]
        n_planes = width // LANES
        for r in range(dil):
            for c in range(n_planes):
                col = r * width + c * LANES
                stage[plane0 + c, pl.ds(r, tm // dil, stride=dil), :] = ref[:, col:col + LANES].astype(F32)
        return jnp.concatenate([stage[plane0 + c] for c in range(n_planes)], axis=1)

    if n_groups == 1:
        o = o_refs[0][...]
    else:
        lses = [token_rows(lse_refs[g], g, LANES, Q_WIDTH // LANES) for g in range(n_groups)]
        mx = functools.reduce(jnp.maximum, lses)
        ex = [jnp.exp(l - mx) for l in lses]
        inv = 1.0 / functools.reduce(jnp.add, ex)
        o = None
        for g in range(n_groups):
            wt = ex[g] * inv
            hi = wt.astype(BF16)
            lo = (wt - hi.astype(F32)).astype(BF16)
            wexp = _dot(jnp.concatenate([hi, lo], axis=-1), e_ref[...])
            term = wexp * token_rows(o_refs[g], g, Q_WIDTH, 0)
            o = term if o is None else o + term
        o = o.astype(BF16)

    x1 = x_ref[...] + _dot(o, wo_ref[...])
    ms = jnp.mean(x1 * x1, axis=-1, keepdims=True)
    h = (x1 * lax.rsqrt(ms + EPS) * g_ref[...]).astype(BF16)
    acc = x1
    for c0, cw in FFN_CHUNKS:
        gate = _dot(h, wgu_ref[:, c0:c0 + cw])
        up = _dot(h, wgu_ref[:, D_FF + c0:D_FF + c0 + cw])
        act = (gate * jax.nn.sigmoid(gate) * up).astype(BF16)
        acc = acc + _dot(act, wd_ref[c0:c0 + cw, :])
    out_ref[...] = acc


def _post_call(os_, dils, lses, expand, x2d, wo, gain, wgu, wd, tm=512):
    m = x2d.shape[0]
    n_groups = len(os_)
    row = lambda width: pl.BlockSpec((tm, width), lambda i: (i, 0))
    in_specs = [pl.BlockSpec((tm // dil, dil * Q_WIDTH), lambda i: (i, 0)) for dil in dils]
    args = list(os_)
    if n_groups > 1:
        in_specs += [pl.BlockSpec((tm // dil, dil * LANES), lambda i: (i, 0)) for dil in dils]
        in_specs += [_resident(expand.shape)]
        args += list(lses) + [expand]
    in_specs += [row(D_MODEL), _resident(wo.shape), _resident(gain.shape), _resident(wgu.shape),
                 _resident(wd.shape)]
    args += [x2d, wo, gain, wgu, wd]
    return pl.pallas_call(
        functools.partial(_post_kernel, dils=tuple(dils)),
        grid=(m // tm,),
        in_specs=in_specs,
        out_specs=row(D_MODEL),
        out_shape=jax.ShapeDtypeStruct((m, D_MODEL), F32),
        scratch_shapes=[pltpu.VMEM((Q_WIDTH // LANES + 1, tm, LANES), F32) for dil in dils if dil > 1],
        compiler_params=pltpu.CompilerParams(dimension_semantics=("parallel",),
                                             vmem_limit_bytes=VMEM_LIMIT),
        name=f"post{n_groups}",
    )(*args)


def _t5_buckets(rel):
    nb = NUM_BUCKETS // 2
    max_exact = nb // 2
    n = np.abs(rel)
    large = max_exact + (np.log(np.maximum(n, 1) / max_exact)
                         / math.log(MAX_DISTANCE / max_exact) * (nb - max_exact)).astype(np.int32)
    large = np.minimum(large, nb - 1)
    return ((rel > 0).astype(np.int32) * nb + np.where(n < max_exact, n, large)).astype(np.int32)


def _bias_table(rel_table, hw, dil):
    blk = ATTN_BLK
    win = blk + 2 * hw
    rel = np.arange(win)[None, :] - hw - np.arange(blk)[:, None]
    band = np.abs(rel) <= hw
    kj = np.arange(win)[None, :]
    masks = []
    for first in (False, True):
        for last in (False, True):
            ok = band.copy()
            if first:
                ok &= kj >= hw
            if last:
                ok &= kj < hw + blk
            masks.append(ok)
    mask = jnp.asarray(np.stack(masks))[:, None, None]
    n = win + blk - 1
    diff = np.arange(n)
    diff = np.where(diff < win, diff, diff - n) - hw
    w = jnp.take(rel_table.astype(F32), jnp.asarray(_t5_buckets(dil * diff)), axis=0).T
    tb = jnp.tile(w, (1, blk))[:, :blk * (n - 1)].reshape(N_HEADS, blk, n - 1)[:, :, :win]
    tb = tb.reshape(1, 2, 8, blk, win)
    bias = jnp.where(mask, tb, NEG)
    return bias.reshape(4, 2, 2, 4 * blk, win).swapaxes(-1, -2)


def _pair_cols(w):
    lead = w.shape[:-1]
    return jnp.swapaxes(w.reshape(*lead, 2, 2, 4, HEAD_DIM), -3, -2).reshape(*lead, Q_WIDTH)


def _qkv_weight(w, n_groups):
    parts = []
    for g in range(n_groups):
        base = g * QKV_WIDTH
        parts.append(_pair_cols(w[:, base:base + Q_WIDTH]))
        parts.append(w[:, base + Q_WIDTH:base + QKV_WIDTH])
    return jnp.concatenate(parts, axis=1).astype(BF16)


def _out_weight(w):
    return jnp.swapaxes(w.reshape(2, 2, 4, HEAD_DIM, D_MODEL), 1, 2).reshape(Q_WIDTH, D_MODEL).astype(BF16)


def _block_diag_ones():
    idx = np.arange(MXU_TILE) // HEAD_DIM
    return jnp.asarray((idx[:, None] == idx[None, :]).astype(np.float32), BF16)


def _expand_matrix():
    col = np.arange(Q_WIDTH)
    p, r, e = col // 512, (col % 512) // LANES, (col % LANES) // HEAD_DIM
    src = e * HEAD_DIM + p * 4 + r
    m = (np.arange(LANES)[:, None] == src[None, :]).astype(np.float32)
    return jnp.asarray(np.concatenate([m, m], axis=0), BF16)


def _trunk(x, p):
    bn, s, _ = x.shape
    m = bn * s
    x2d = x.reshape(m, D_MODEL)

    q, k, v = _qkv_call(x2d, p["norm_attn"][0], p["a_w"], p["bd"], p["a_qg"], p["a_kg"], (1,))
    (o,) = _attn_call(q.reshape(bn, s, Q_WIDTH), k.reshape(bn, s, KV_WIDTH), v.reshape(bn, s, KV_WIDTH),
                      p["a_bias"], p["a_sink"], 1, A_HALF_WINDOW, False)
    x2d = _post_call([o.reshape(m, Q_WIDTH)], (1,), None, None, x2d, p["a_wo"], p["norm_ffn"][0],
                     p["wgu"][0], p["wd"][0])

    dils = tuple(d for _, d in DILATED_GROUPS)
    outs = _qkv_call(x2d, p["norm_attn"][1], p["b_w"], p["bd"], p["b_qg"], p["b_kg"], dils)
    os_, lses = [], []
    for gi, (window, dil) in enumerate(DILATED_GROUPS):
        q, k, v = outs[3 * gi:3 * gi + 3]
        ls = s // dil
        o, lse = _attn_call(q.reshape(bn, ls, dil * Q_WIDTH), k.reshape(bn, ls, dil * KV_WIDTH),
                            v.reshape(bn, ls, dil * KV_WIDTH), p["b_bias"][gi], None, dil,
                            window // (2 * dil), True)
        os_.append(o.reshape(m // dil, dil * Q_WIDTH))
        lses.append(lse.reshape(m // dil, dil * LANES))
    x2d = _post_call(os_, dils, lses, p["expand"], x2d, p["b_wo"], p["norm_ffn"][1], p["wgu"][1], p["wd"][1])
    return x2d.reshape(bn, s, D_MODEL)


def kernel(x_prompt, x_sample, rel_table, norm_attn, norm_ffn, a_w_qkv, a_q_gain, a_k_gain, a_sink, a_w_o,
           b_w_qkv, b_q_gain, b_k_gain, b_w_o, ffn_w_gate_up, ffn_w_down):
    n_grp = len(DILATED_GROUPS)
    scale = HEAD_DIM ** -0.5
    blk = ATTN_BLK
    sink_rows = jnp.broadcast_to(a_sink[0].astype(F32).reshape(2, 2, 1, 4, 1),
                                 (2, 2, 1, 4, blk)).reshape(2, 2, 1, 4 * blk)
    p = {
        "norm_attn": norm_attn.astype(F32).reshape(-1, 1, D_MODEL),
        "norm_ffn": norm_ffn.astype(F32).reshape(-1, 1, D_MODEL),
        "bd": _block_diag_ones(),
        "expand": _expand_matrix(),
        "a_w": _qkv_weight(a_w_qkv[0], 1),
        "a_qg": (jnp.tile(a_q_gain[0].astype(F32), N_HEADS) * scale).reshape(1, 1, Q_WIDTH),
        "a_kg": jnp.tile(a_k_gain[0].astype(F32), N_KV_HEADS).reshape(1, 1, KV_WIDTH),
        "a_bias": _bias_table(rel_table, A_HALF_WINDOW, 1),
        "a_sink": sink_rows,
        "a_wo": _out_weight(a_w_o[0]),
        "b_w": _qkv_weight(b_w_qkv[0], n_grp),
        "b_qg": (jnp.tile(b_q_gain[0].astype(F32), (1, N_HEADS)) * scale).reshape(n_grp, 1, Q_WIDTH),
        "b_kg": jnp.tile(b_k_gain[0].astype(F32), (1, N_KV_HEADS)).reshape(n_grp, 1, KV_WIDTH),
        "b_bias": [_bias_table(rel_table, w // (2 * d), d) for w, d in DILATED_GROUPS],
        "b_wo": _out_weight(b_w_o[0]),
        "wgu": ffn_w_gate_up.astype(BF16),
        "wd": ffn_w_down.astype(BF16),
    }
    return _trunk(x_prompt, p), _trunk(x_sample, p)
```

```python
import functools
import math

import numpy as np
import jax
import jax.numpy as jnp
from jax import lax
from jax.experimental import pallas as pl
from jax.experimental.pallas import tpu as pltpu

D_MODEL = 1024
N_HEADS = 16
N_KV_HEADS = 4
HEAD_DIM = 64
Q_WIDTH = N_HEADS * HEAD_DIM
KV_WIDTH = N_KV_HEADS * HEAD_DIM
QKV_WIDTH = Q_WIDTH + 2 * KV_WIDTH
D_FF = 2816
A_HALF_WINDOW = 128
DILATED_GROUPS = ((128, 1), (512, 4), (2048, 16))
NUM_BUCKETS = 32
MAX_DISTANCE = 1024
EPS = 1e-6
NEG = -1e30

LANES = 128
MXU_TILE = 256
ATTN_BLK = 128
ATTN_STEP_TOKENS = 1024
VMEM_LIMIT = 56 * 1024 * 1024

BF16 = jnp.bfloat16
F32 = jnp.float32


def _dot(a, b):
    return jnp.dot(a, b, preferred_element_type=F32)


def _dot_nt(a, b):
    return lax.dot_general(a, b, (((1,), (1,)), ((), ())), preferred_element_type=F32)


def _dot_tn(a, b):
    return lax.dot_general(a, b, (((0,), (0,)), ((), ())), preferred_element_type=F32)


def _resident(shape):
    nd = len(shape)
    return pl.BlockSpec(shape, lambda *_: (0,) * nd, pipeline_mode=pl.Buffered(1))


def _qkv_kernel(x_ref, g_ref, w_ref, bd_ref, qg_ref, kg_ref, *refs, dils):
    out_refs, stages = refs[:-1], refs[-1]
    tm = x_ref.shape[0]
    x = x_ref[...]
    ms = jnp.mean(x * x, axis=-1, keepdims=True)
    h = (x * lax.rsqrt(ms + EPS) * g_ref[...]).astype(BF16)
    bd = bd_ref[...]

    def emit(ref, dil, width, lo, val, stage):
        if dil == 1:
            ref[:, lo:lo + MXU_TILE] = val.astype(BF16)
            return
        for c in range(MXU_TILE // LANES):
            stage[c] = val[:, c * LANES:(c + 1) * LANES]
        for r in range(dil):
            for c in range(MXU_TILE // LANES):
                col = r * width + lo + c * LANES
                ref[:, col:col + LANES] = stage[c, pl.ds(r, tm // dil, stride=dil), :].astype(BF16)

    n_cols = QKV_WIDTH // MXU_TILE
    n_norm = (Q_WIDTH + KV_WIDTH) // MXU_TILE
    units = [(g, c) for g in range(len(dils)) for c in range(n_cols)]
    proj, sumsq = {}, {}

    def project(u):
        g, c = u
        lo = g * QKV_WIDTH + c * MXU_TILE
        proj[u] = _dot(h, w_ref[:, lo:lo + MXU_TILE])

    def square_sum(u):
        if u[1] < n_norm:
            t = proj[u]
            sumsq[u] = _dot((t * t).astype(BF16), bd)

    def write(i):
        g, c = u = units[i]
        q_ref, k_ref, v_ref = out_refs[3 * g:3 * g + 3]
        val = proj.pop(u)
        lo = c * MXU_TILE
        if c < n_norm:
            gain = qg_ref[g, :, lo:lo + MXU_TILE] if lo < Q_WIDTH else kg_ref[g]
            val = val * lax.rsqrt(sumsq.pop(u) * (1.0 / HEAD_DIM) + EPS) * gain
        ref, width, lo = ((q_ref, Q_WIDTH, lo) if lo < Q_WIDTH else
                          (k_ref, KV_WIDTH, 0) if c < n_norm else (v_ref, KV_WIDTH, 0))
        emit(ref, dils[g], width, lo, val, stages.at[i % 2])

    n = len(units)
    for t in range(n + 2):
        if t < n:
            project(units[t])
        if 0 <= t - 1 < n:
            square_sum(units[t - 1])
        if 0 <= t - 2 < n:
            write(t - 2)


def _qkv_call(x2d, gain, w, bd, qg, kg, dils, tm=512):
    m = x2d.shape[0]
    out_shape, out_specs = [], []
    for dil in dils:
        for width in (Q_WIDTH, KV_WIDTH, KV_WIDTH):
            out_shape.append(jax.ShapeDtypeStruct((m // dil, dil * width), BF16))
            out_specs.append(pl.BlockSpec((tm // dil, dil * width), lambda i: (i, 0)))
    return pl.pallas_call(
        functools.partial(_qkv_kernel, dils=tuple(dils)),
        grid=(m // tm,),
        in_specs=[pl.BlockSpec((tm, D_MODEL), lambda i: (i, 0)), _resident(gain.shape), _resident(w.shape),
                  _resident(bd.shape), _resident(qg.shape), _resident(kg.shape)],
        out_specs=out_specs,
        out_shape=out_shape,
        scratch_shapes=[pltpu.VMEM((2, MXU_TILE // LANES, tm, LANES), F32)],
        compiler_params=pltpu.CompilerParams(dimension_semantics=("parallel",),
                                             vmem_limit_bytes=VMEM_LIMIT),
        name=f"qkv{len(dils)}",
    )(x2d, gain, w, bd, qg, kg)


def _attn_kernel(*refs, tq, hw, n_res, n_blocks, has_sink, has_lse):
    q_ref, kl_ref, kc_ref, kr_ref, vl_ref, vc_ref, vr_ref, bias_ref = refs[:8]
    refs = refs[8:]
    sink_ref = None
    if has_sink:
        sink_ref, refs = refs[0], refs[1:]
    o_ref, refs = refs[0], refs[1:]
    lse_ref = None
    if has_lse:
        lse_ref, refs = refs[0], refs[1:]
    kw, vw = refs

    blk = ATTN_BLK
    win = blk + 2 * hw

    for dst, (l, c, r) in ((kw, (kl_ref, kc_ref, kr_ref)), (vw, (vl_ref, vc_ref, vr_ref))):
        dst[0:hw, :] = l[...]
        dst[hw:hw + tq, :] = c[...]
        dst[hw + tq:hw + tq + hw, :] = r[...]

    tile = pl.program_id(2)
    n_inner = tq // blk
    lane = lax.broadcasted_iota(jnp.int32, (1, LANES), 1)
    keep = (lane < HEAD_DIM, lane >= HEAD_DIM)
    first_half = lax.broadcasted_iota(jnp.int32, (LANES, 1), 0) < HEAD_DIM
    sub = lax.broadcasted_iota(jnp.int32, (8, 1), 0)

    chunks = [(j, u, p, e) for u in range(n_res) for j in range(n_inner) for p in range(2) for e in range(2)]
    kinds, lse_rows = {}, {}
    for j in range(n_inner):
        gb = tile * n_inner + j
        kinds[j] = 2 * (gb == 0).astype(jnp.int32) + (gb == n_blocks - 1).astype(jnp.int32)
    scores_, probs_, results_ = {}, {}, {}

    def scores(c):
        j, u, p, e = c
        kp = kw[j * blk:j * blk + win, u * KV_WIDTH + p * LANES:u * KV_WIDTH + (p + 1) * LANES]
        base = u * Q_WIDTH + p * 512
        qcat = jnp.concatenate([q_ref[j * blk:(j + 1) * blk, base + r * LANES:base + (r + 1) * LANES]
                                for r in range(4)], axis=0)
        k_e = jnp.where(keep[e], kp, jnp.zeros((), BF16))
        scores_[c] = _dot_nt(k_e, qcat) + bias_ref[kinds[j], p, e]

    def probs(c):
        j, u, p, e = c
        st = scores_.pop(c)
        m = jnp.max(st, axis=0, keepdims=True)
        sk = None
        if has_sink:
            sk = sink_ref[p, e]
            m = jnp.maximum(m, sk)
        probs_[c] = (jnp.exp((st - m).astype(BF16)), m, sk)

    def values(c):
        j, u, p, e = c
        pt, m, sk = probs_.pop(c)
        vp = vw[j * blk:j * blk + win, u * KV_WIDTH + p * LANES:u * KV_WIDTH + (p + 1) * LANES]
        v_e = jnp.where(keep[e], vp, jnp.ones((), BF16))
        ot = _dot_tn(v_e, pt)
        den = ot[HEAD_DIM:HEAD_DIM + 1] if e == 0 else ot[0:1]
        if has_sink:
            den = den + jnp.exp(sk - m)
        results_[c] = (ot, 1.0 / den, (m + jnp.log(den)) if has_lse else None)
        if e == 1:
            finish(j, u, p)

    def finish(j, u, p):
        (o0, i0, l0), (o1, i1, l1) = results_.pop((j, u, p, 0)), results_.pop((j, u, p, 1))
        zeros = jnp.zeros((8, blk), F32)
        lse_lo, lse_hi = lse_rows.get((j, u), (zeros, zeros))
        for r in range(4):
            cs = slice(r * blk, (r + 1) * blk)
            t = jnp.where(first_half, o0[:, cs] * i0[:, cs], o1[:, cs] * i1[:, cs])
            col = u * Q_WIDTH + p * 512 + r * LANES
            o_ref[j * blk:(j + 1) * blk, col:col + LANES] = t.T.astype(BF16)
            if has_lse:
                lse_lo = jnp.where(sub == p * 4 + r, l0[:, cs], lse_lo)
                lse_hi = jnp.where(sub == p * 4 + r, l1[:, cs], lse_hi)
        lse_rows[(j, u)] = (lse_lo, lse_hi)
        if has_lse and p == 1:
            z = jnp.zeros((HEAD_DIM - 8, blk), F32)
            lse_ref[j * blk:(j + 1) * blk, u * LANES:(u + 1) * LANES] = (
                jnp.concatenate([lse_lo, z, lse_hi, z], axis=0).T)

    n = len(chunks)
    for t in range(n + 2):
        if t < n:
            scores(chunks[t])
        if 0 <= t - 1 < n:
            probs(chunks[t - 1])
        if 0 <= t - 2 < n:
            values(chunks[t - 2])


def _attn_call(q, k, v, bias, sink_rows, dil, hw, has_lse):
    bn, ls, _ = q.shape
    blk = ATTN_BLK
    tq = min(ATTN_STEP_TOKENS, ls)
    n_res = ATTN_STEP_TOKENS // tq
    assert ls % tq == 0 and tq % blk == 0 and tq % hw == 0 and blk % hw == 0 and dil % n_res == 0
    n_tiles = ls // tq
    ratio = tq // hw
    last_halo = ls // hw - 1

    center = lambda b, r, i: (b, i, r)
    left = lambda b, r, i: (b, jnp.maximum(i * ratio - 1, 0), r)
    right = lambda b, r, i: (b, jnp.minimum((i + 1) * ratio, last_halo), r)
    kv_specs = [pl.BlockSpec((None, hw, n_res * KV_WIDTH), left),
                pl.BlockSpec((None, tq, n_res * KV_WIDTH), center),
                pl.BlockSpec((None, hw, n_res * KV_WIDTH), right)]
    in_specs = ([pl.BlockSpec((None, tq, n_res * Q_WIDTH), center)] + kv_specs + kv_specs
                + [_resident(bias.shape)])
    args = [q, k, k, k, v, v, v, bias]
    if sink_rows is not None:
        in_specs.append(_resident(sink_rows.shape))
        args.append(sink_rows)
    out_shape = [jax.ShapeDtypeStruct(q.shape, BF16)]
    out_specs = [pl.BlockSpec((None, tq, n_res * Q_WIDTH), center)]
    if has_lse:
        out_shape.append(jax.ShapeDtypeStruct((bn, ls, dil * LANES), F32))
        out_specs.append(pl.BlockSpec((None, tq, n_res * LANES), center))
    win_rows = tq + 2 * hw
    return pl.pallas_call(
        functools.partial(_attn_kernel, tq=tq, hw=hw, n_res=n_res, n_blocks=ls // blk,
                          has_sink=sink_rows is not None, has_lse=has_lse),
        grid=(bn, dil // n_res, n_tiles),
        in_specs=in_specs,
        out_specs=out_specs,
        out_shape=out_shape,
        scratch_shapes=[pltpu.VMEM((win_rows, n_res * KV_WIDTH), BF16),
                        pltpu.VMEM((win_rows, n_res * KV_WIDTH), BF16)],
        compiler_params=pltpu.CompilerParams(dimension_semantics=("parallel", "parallel", "parallel"),
                                             vmem_limit_bytes=VMEM_LIMIT),
        name=f"attn_d{dil}",
    )(*args)


FFN_CHUNKS = ((0, 1536), (1536, 1280))


def _post_kernel(*refs, dils):
    n_groups = len(dils)
    o_refs, refs = refs[:n_groups], refs[n_groups:]
    if n_groups > 1:
        lse_refs, refs = refs[:n_groups], refs[n_groups:]
        e_ref, refs = refs[0], refs[1:]
    x_ref, wo_ref, g_ref, wgu_ref, wd_ref, out_ref = refs[:6]
    stages = dict(zip([g for g, dil in enumerate(dils) if dil > 1], refs[6:]))
    tm = x_ref.shape[0]

    def token_rows(ref, g, width, plane0):
        dil = dils[g]
        if dil == 1:
            return ref[...].astype(F32)
        stage = stages[g]
        n_planes = width // LANES
        for r in range(dil):
            for c in range(n_planes):
                col = r * width + c * LANES
                stage[plane0 + c, pl.ds(r, tm // dil, stride=dil), :] = ref[:, col:col + LANES].astype(F32)
        return jnp.concatenate([stage[plane0 + c] for c in range(n_planes)], axis=1)

    if n_groups == 1:
        o = o_refs[0][...]
    else:
        lses = [token_rows(lse_refs[g], g, LANES, Q_WIDTH // LANES) for g in range(n_groups)]
        mx = functools.reduce(jnp.maximum, lses)
        ex = [jnp.exp(l - mx) for l in lses]
        inv = 1.0 / functools.reduce(jnp.add, ex)
        o = None
        for g in range(n_groups):
            wt = ex[g] * inv
            hi = wt.astype(BF16)
            lo = (wt - hi.astype(F32)).astype(BF16)
            wexp = _dot(jnp.concatenate([hi, lo], axis=-1), e_ref[...])
            term = wexp * token_rows(o_refs[g], g, Q_WIDTH, 0)
            o = term if o is None else o + term
        o = o.astype(BF16)

    x1 = x_ref[...] + _dot(o, wo_ref[...])
    ms = jnp.mean(x1 * x1, axis=-1, keepdims=True)
    h = (x1 * lax.rsqrt(ms + EPS) * g_ref[...]).astype(BF16)
    acc = x1
    for c0, cw in FFN_CHUNKS:
        gate = _dot(h, wgu_ref[:, c0:c0 + cw])
        up = _dot(h, wgu_ref[:, D_FF + c0:D_FF + c0 + cw])
        act = (gate * jax.nn.sigmoid(gate) * up).astype(BF16)
        acc = acc + _dot(act, wd_ref[c0:c0 + cw, :])
    out_ref[...] = acc


def _post_call(os_, dils, lses, expand, x2d, wo, gain, wgu, wd, tm=512):
    m = x2d.shape[0]
    n_groups = len(os_)
    row = lambda width: pl.BlockSpec((tm, width), lambda i: (i, 0))
    in_specs = [pl.BlockSpec((tm // dil, dil * Q_WIDTH), lambda i: (i, 0)) for dil in dils]
    args = list(os_)
    if n_groups > 1:
        in_specs += [pl.BlockSpec((tm // dil, dil * LANES), lambda i: (i, 0)) for dil in dils]
        in_specs += [_resident(expand.shape)]
        args += list(lses) + [expand]
    in_specs += [row(D_MODEL), _resident(wo.shape), _resident(gain.shape), _resident(wgu.shape),
                 _resident(wd.shape)]
    args += [x2d, wo, gain, wgu, wd]
    return pl.pallas_call(
        functools.partial(_post_kernel, dils=tuple(dils)),
        grid=(m // tm,),
        in_specs=in_specs,
        out_specs=row(D_MODEL),
        out_shape=jax.ShapeDtypeStruct((m, D_MODEL), F32),
        scratch_shapes=[pltpu.VMEM((Q_WIDTH // LANES + 1, tm, LANES), F32) for dil in dils if dil > 1],
        compiler_params=pltpu.CompilerParams(dimension_semantics=("parallel",),
                                             vmem_limit_bytes=VMEM_LIMIT),
        name=f"post{n_groups}",
    )(*args)


def _t5_buckets(rel):
    nb = NUM_BUCKETS // 2
    max_exact = nb // 2
    n = np.abs(rel)
    large = max_exact + (np.log(np.maximum(n, 1) / max_exact)
                         / math.log(MAX_DISTANCE / max_exact) * (nb - max_exact)).astype(np.int32)
    large = np.minimum(large, nb - 1)
    return ((rel > 0).astype(np.int32) * nb + np.where(n < max_exact, n, large)).astype(np.int32)


def _bias_table(rel_table, hw, dil):
    blk = ATTN_BLK
    win = blk + 2 * hw
    rel = np.arange(win)[None, :] - hw - np.arange(blk)[:, None]
    band = np.abs(rel) <= hw
    kj = np.arange(win)[None, :]
    masks = []
    for first in (False, True):
        for last in (False, True):
            ok = band.copy()
            if first:
                ok &= kj >= hw
            if last:
                ok &= kj < hw + blk
            masks.append(ok)
    mask = jnp.asarray(np.stack(masks))[:, None, None]
    n = win + blk - 1
    diff = np.arange(n)
    diff = np.where(diff < win, diff, diff - n) - hw
    w = jnp.take(rel_table.astype(F32), jnp.asarray(_t5_buckets(dil * diff)), axis=0).T
    tb = jnp.tile(w, (1, blk))[:, :blk * (n - 1)].reshape(N_HEADS, blk, n - 1)[:, :, :win]
    tb = tb.reshape(1, 2, 8, blk, win)
    bias = jnp.where(mask, tb, NEG)
    return bias.reshape(4, 2, 2, 4 * blk, win).swapaxes(-1, -2)


def _pair_cols(w):
    lead = w.shape[:-1]
    return jnp.swapaxes(w.reshape(*lead, 2, 2, 4, HEAD_DIM), -3, -2).reshape(*lead, Q_WIDTH)


def _qkv_weight(w, n_groups):
    parts = []
    for g in range(n_groups):
        base = g * QKV_WIDTH
        parts.append(_pair_cols(w[:, base:base + Q_WIDTH]))
        parts.append(w[:, base + Q_WIDTH:base + QKV_WIDTH])
    return jnp.concatenate(parts, axis=1).astype(BF16)


def _out_weight(w):
    return jnp.swapaxes(w.reshape(2, 2, 4, HEAD_DIM, D_MODEL), 1, 2).reshape(Q_WIDTH, D_MODEL).astype(BF16)


def _block_diag_ones():
    idx = np.arange(MXU_TILE) // HEAD_DIM
    return jnp.asarray((idx[:, None] == idx[None, :]).astype(np.float32), BF16)


def _expand_matrix():
    col = np.arange(Q_WIDTH)
    p, r, e = col // 512, (col % 512) // LANES, (col % LANES) // HEAD_DIM
    src = e * HEAD_DIM + p * 4 + r
    m = (np.arange(LANES)[:, None] == src[None, :]).astype(np.float32)
    return jnp.asarray(np.concatenate([m, m], axis=0), BF16)


def _trunk(x, p):
    bn, s, _ = x.shape
    m = bn * s
    x2d = x.reshape(m, D_MODEL)

    q, k, v = _qkv_call(x2d, p["norm_attn"][0], p["a_w"], p["bd"], p["a_qg"], p["a_kg"], (1,))
    (o,) = _attn_call(q.reshape(bn, s, Q_WIDTH), k.reshape(bn, s, KV_WIDTH), v.reshape(bn, s, KV_WIDTH),
                      p["a_bias"], p["a_sink"], 1, A_HALF_WINDOW, False)
    x2d = _post_call([o.reshape(m, Q_WIDTH)], (1,), None, None, x2d, p["a_wo"], p["norm_ffn"][0],
                     p["wgu"][0], p["wd"][0])

    dils = tuple(d for _, d in DILATED_GROUPS)
    outs = _qkv_call(x2d, p["norm_attn"][1], p["b_w"], p["bd"], p["b_qg"], p["b_kg"], dils)
    os_, lses = [], []
    for gi, (window, dil) in enumerate(DILATED_GROUPS):
        q, k, v = outs[3 * gi:3 * gi + 3]
        ls = s // dil
        o, lse = _attn_call(q.reshape(bn, ls, dil * Q_WIDTH), k.reshape(bn, ls, dil * KV_WIDTH),
                            v.reshape(bn, ls, dil * KV_WIDTH), p["b_bias"][gi], None, dil,
                            window // (2 * dil), True)
        os_.append(o.reshape(m // dil, dil * Q_WIDTH))
        lses.append(lse.reshape(m // dil, dil * LANES))
    x2d = _post_call(os_, dils, lses, p["expand"], x2d, p["b_wo"], p["norm_ffn"][1], p["wgu"][1], p["wd"][1])
    return x2d.reshape(bn, s, D_MODEL)


def kernel(x_prompt, x_sample, rel_table, norm_attn, norm_ffn, a_w_qkv, a_q_gain, a_k_gain, a_sink, a_w_o,
           b_w_qkv, b_q_gain, b_k_gain, b_w_o, ffn_w_gate_up, ffn_w_down):
    n_grp = len(DILATED_GROUPS)
    scale = HEAD_DIM ** -0.5
    blk = ATTN_BLK
    sink_rows = jnp.broadcast_to(a_sink[0].astype(F32).reshape(2, 2, 1, 4, 1),
                                 (2, 2, 1, 4, blk)).reshape(2, 2, 1, 4 * blk)
    p = {
        "norm_attn": norm_attn.astype(F32).reshape(-1, 1, D_MODEL),
        "norm_ffn": norm_ffn.astype(F32).reshape(-1, 1, D_MODEL),
        "bd": _block_diag_ones(),
        "expand": _expand_matrix(),
        "a_w": _qkv_weight(a_w_qkv[0], 1),
        "a_qg": (jnp.tile(a_q_gain[0].astype(F32), N_HEADS) * scale).reshape(1, 1, Q_WIDTH),
        "a_kg": jnp.tile(a_k_gain[0].astype(F32), N_KV_HEADS).reshape(1, 1, KV_WIDTH),
        "a_bias": _bias_table(rel_table, A_HALF_WINDOW, 1),
        "a_sink": sink_rows,
        "a_wo": _out_weight(a_w_o[0]),
        "b_w": _qkv_weight(b_w_qkv[0], n_grp),
        "b_qg": (jnp.tile(b_q_gain[0].astype(F32), (1, N_HEADS)) * scale).reshape(n_grp, 1, Q_WIDTH),
        "b_kg": jnp.tile(b_k_gain[0].astype(F32), (1, N_KV_HEADS)).reshape(n_grp, 1, KV_WIDTH),
        "b_bias": [_bias_table(rel_table, w // (2 * d), d) for w, d in DILATED_GROUPS],
        "b_wo": _out_weight(b_w_o[0]),
        "wgu": ffn_w_gate_up.astype(BF16),
        "wd": ffn_w_down.astype(BF16),
    }
    return _trunk(x_prompt, p), _trunk(x_sample, p)
```

```python
import functools
import math

import numpy as np
import jax
import jax.numpy as jnp
from jax import lax
from jax.experimental import pallas as pl
from jax.experimental.pallas import tpu as pltpu

D_MODEL = 1024
N_HEADS = 16
N_KV_HEADS = 4
HEAD_DIM = 64
Q_WIDTH = N_HEADS * HEAD_DIM
KV_WIDTH = N_KV_HEADS * HEAD_DIM
QKV_WIDTH = Q_WIDTH + 2 * KV_WIDTH
D_FF = 2816
A_HALF_WINDOW = 128
DILATED_GROUPS = ((128, 1), (512, 4), (2048, 16))
NUM_BUCKETS = 32
MAX_DISTANCE = 1024
EPS = 1e-6
NEG = -1e30

LANES = 128
MXU_TILE = 256
ATTN_BLK = 128
ATTN_HALF = 64
ATTN_STEP_TOKENS = 1024
ATTN_OFFSETS = (2, 4)
VMEM_LIMIT = 56 * 1024 * 1024

BF16 = jnp.bfloat16
F32 = jnp.float32


def _dot(a, b):
    return jnp.dot(a, b, preferred_element_type=F32)


def _dot_nt(a, b):
    return lax.dot_general(a, b, (((1,), (1,)), ((), ())), preferred_element_type=F32)


def _dot_tn(a, b):
    return lax.dot_general(a, b, (((0,), (0,)), ((), ())), preferred_element_type=F32)


def _resident(shape):
    nd = len(shape)
    return pl.BlockSpec(shape, lambda *_: (0,) * nd, pipeline_mode=pl.Buffered(1))


def _qkv_kernel(x_ref, g_ref, w_ref, bd_ref, qg_ref, kg_ref, *refs, dils):
    out_refs, stages = refs[:-1], refs[-1]
    tm = x_ref.shape[0]
    x = x_ref[...]
    ms = jnp.mean(x * x, axis=-1, keepdims=True)
    h = (x * lax.rsqrt(ms + EPS) * g_ref[...]).astype(BF16)
    bd = bd_ref[...]

    def emit(ref, dil, width, lo, val, stage):
        if dil == 1:
            ref[:, lo:lo + MXU_TILE] = val.astype(BF16)
            return
        for c in range(MXU_TILE // LANES):
            stage[c] = val[:, c * LANES:(c + 1) * LANES]
        for r in range(dil):
            for c in range(MXU_TILE // LANES):
                col = r * width + lo + c * LANES
                ref[:, col:col + LANES] = stage[c, pl.ds(r, tm // dil, stride=dil), :].astype(BF16)

    n_cols = QKV_WIDTH // MXU_TILE
    n_norm = (Q_WIDTH + KV_WIDTH) // MXU_TILE
    units = [(g, c) for g in range(len(dils)) for c in range(n_cols)]
    proj, sumsq = {}, {}

    def project(u):
        g, c = u
        lo = g * QKV_WIDTH + c * MXU_TILE
        proj[u] = _dot(h, w_ref[:, lo:lo + MXU_TILE])

    def square_sum(u):
        if u[1] < n_norm:
            t = proj[u]
            sumsq[u] = _dot((t * t).astype(BF16), bd)

    def write(i):
        g, c = u = units[i]
        q_ref, k_ref, v_ref = out_refs[3 * g:3 * g + 3]
        val = proj.pop(u)
        lo = c * MXU_TILE
        if c < n_norm:
            gain = qg_ref[g, :, lo:lo + MXU_TILE] if lo < Q_WIDTH else kg_ref[g]
            val = val * lax.rsqrt(sumsq.pop(u) * (1.0 / HEAD_DIM) + EPS) * gain
        ref, width, lo = ((q_ref, Q_WIDTH, lo) if lo < Q_WIDTH else
                          (k_ref, KV_WIDTH, 0) if c < n_norm else (v_ref, KV_WIDTH, 0))
        emit(ref, dils[g], width, lo, val, stages.at[i % 2])

    n = len(units)
    for t in range(n + 2):
        if t < n:
            project(units[t])
        if 0 <= t - 1 < n:
            square_sum(units[t - 1])
        if 0 <= t - 2 < n:
            write(t - 2)


def _qkv_call(x2d, gain, w, bd, qg, kg, dils, tm=512):
    m = x2d.shape[0]
    out_shape, out_specs = [], []
    for dil in dils:
        for width in (Q_WIDTH, KV_WIDTH, KV_WIDTH):
            out_shape.append(jax.ShapeDtypeStruct((m // dil, dil * width), BF16))
            out_specs.append(pl.BlockSpec((tm // dil, dil * width), lambda i: (i, 0)))
    return pl.pallas_call(
        functools.partial(_qkv_kernel, dils=tuple(dils)),
        grid=(m // tm,),
        in_specs=[pl.BlockSpec((tm, D_MODEL), lambda i: (i, 0)), _resident(gain.shape), _resident(w.shape),
                  _resident(bd.shape), _resident(qg.shape), _resident(kg.shape)],
        out_specs=out_specs,
        out_shape=out_shape,
        scratch_shapes=[pltpu.VMEM((2, MXU_TILE // LANES, tm, LANES), F32)],
        compiler_params=pltpu.CompilerParams(dimension_semantics=("parallel",),
                                             vmem_limit_bytes=VMEM_LIMIT),
        name=f"qkv{len(dils)}",
    )(x2d, gain, w, bd, qg, kg)


def _attn_kernel(*refs, tq, hw, n_res, n_blocks, has_sink, has_lse):
    q_ref, kl_ref, kc_ref, kr_ref, vl_ref, vc_ref, vr_ref, bias_ref = refs[:8]
    refs = refs[8:]
    sink_ref = None
    if has_sink:
        sink_ref, refs = refs[0], refs[1:]
    o_ref, refs = refs[0], refs[1:]
    lse_ref = None
    if has_lse:
        lse_ref, refs = refs[0], refs[1:]
    kw, vw = refs

    blk = ATTN_BLK
    half = ATTN_HALF
    win = half + 2 * hw

    for dst, (l, c, r) in ((kw, (kl_ref, kc_ref, kr_ref)), (vw, (vl_ref, vc_ref, vr_ref))):
        dst[0:hw, :] = l[...]
        dst[hw:hw + tq, :] = c[...]
        dst[hw + tq:hw + tq + hw, :] = r[...]

    tile = pl.program_id(2)
    n_inner = tq // blk
    lane = lax.broadcasted_iota(jnp.int32, (1, LANES), 1)
    keep = (lane < HEAD_DIM, lane >= HEAD_DIM)
    first_half = lax.broadcasted_iota(jnp.int32, (LANES, 1), 0) < HEAD_DIM
    sub = lax.broadcasted_iota(jnp.int32, (8, 1), 0)

    chunks = [(j, u, p, h, e) for u in range(n_res) for j in range(n_inner) for p in range(2)
              for h in range(2) for e in range(2)]
    kinds, lse_rows = {}, {}
    for j in range(n_inner):
        gb = tile * n_inner + j
        kinds[j] = 2 * (gb == 0).astype(jnp.int32) + (gb == n_blocks - 1).astype(jnp.int32)
    scores_, probs_, results_ = {}, {}, {}

    def scores(c):
        j, u, p, h, e = c
        r0 = j * blk + h * half
        kp = kw[r0:r0 + win, u * KV_WIDTH + p * LANES:u * KV_WIDTH + (p + 1) * LANES]
        base = u * Q_WIDTH + p * 512
        qcat = jnp.concatenate([q_ref[r0:r0 + half, base + r * LANES:base + (r + 1) * LANES]
                                for r in range(4)], axis=0)
        k_e = jnp.where(keep[e], kp, jnp.zeros((), BF16))
        scores_[c] = _dot_nt(k_e, qcat) + bias_ref[kinds[j], p, e, h]

    def probs(c):
        j, u, p, h, e = c
        st = scores_.pop(c)
        m = jnp.max(st, axis=0, keepdims=True)
        sk = None
        if has_sink:
            sk = sink_ref[p, e]
            m = jnp.maximum(m, sk)
        probs_[c] = (jnp.exp((st - m).astype(BF16)), m, sk)

    def values(c):
        j, u, p, h, e = c
        pt, m, sk = probs_.pop(c)
        r0 = j * blk + h * half
        vp = vw[r0:r0 + win, u * KV_WIDTH + p * LANES:u * KV_WIDTH + (p + 1) * LANES]
        v_e = jnp.where(keep[e], vp, jnp.ones((), BF16))
        ot = _dot_tn(v_e, pt)
        den = ot[HEAD_DIM:HEAD_DIM + 1] if e == 0 else ot[0:1]
        if has_sink:
            den = den + jnp.exp(sk - m)
        results_[c] = (ot, 1.0 / den, (m + jnp.log(den)) if has_lse else None)
        if e == 1:
            finish(j, u, p, h)

    def finish(j, u, p, h):
        (o0, i0, l0), (o1, i1, l1) = results_.pop((j, u, p, h, 0)), results_.pop((j, u, p, h, 1))
        zeros = jnp.zeros((8, blk), F32)
        lse_lo, lse_hi = lse_rows.get((j, u), (zeros, zeros))
        r0 = j * blk + h * half
        for rp in range(2):
            cs = slice(rp * LANES, (rp + 1) * LANES)
            t = jnp.where(first_half, o0[:, cs] * i0[:, cs], o1[:, cs] * i1[:, cs])
            tt = t.T.astype(BF16)
            for s in range(2):
                col = u * Q_WIDTH + p * 512 + (2 * rp + s) * LANES
                o_ref[r0:r0 + half, col:col + LANES] = tt[s * half:(s + 1) * half]
            if has_lse:
                v0, v1 = l0[:, cs], l1[:, cs]
                w0, w1 = pltpu.roll(v0, HEAD_DIM, axis=1), pltpu.roll(v1, HEAD_DIM, axis=1)
                for s in range(2):
                    mask = (sub == p * 4 + 2 * rp + s) & keep[h]
                    lse_lo = jnp.where(mask, v0 if s == h else w0, lse_lo)
                    lse_hi = jnp.where(mask, v1 if s == h else w1, lse_hi)
        lse_rows[(j, u)] = (lse_lo, lse_hi)
        if has_lse and p == 1 and h == 1:
            z = jnp.zeros((HEAD_DIM - 8, blk), F32)
            lse_ref[j * blk:(j + 1) * blk, u * LANES:(u + 1) * LANES] = (
                jnp.concatenate([lse_lo, z, lse_hi, z], axis=0).T)

    n = len(chunks)
    d1, d2 = ATTN_OFFSETS
    for t in range(n + d2):
        if t < n:
            scores(chunks[t])
        if 0 <= t - d1 < n:
            probs(chunks[t - d1])
        if 0 <= t - d2 < n:
            values(chunks[t - d2])


def _attn_call(q, k, v, bias, sink_rows, dil, hw, has_lse):
    bn, ls, _ = q.shape
    blk = ATTN_BLK
    tq = min(ATTN_STEP_TOKENS, ls)
    n_res = ATTN_STEP_TOKENS // tq
    assert ls % tq == 0 and tq % blk == 0 and tq % hw == 0 and blk % hw == 0 and dil % n_res == 0
    n_tiles = ls // tq
    ratio = tq // hw
    last_halo = ls // hw - 1

    center = lambda b, r, i: (b, i, r)
    left = lambda b, r, i: (b, jnp.maximum(i * ratio - 1, 0), r)
    right = lambda b, r, i: (b, jnp.minimum((i + 1) * ratio, last_halo), r)
    kv_specs = [pl.BlockSpec((None, hw, n_res * KV_WIDTH), left),
                pl.BlockSpec((None, tq, n_res * KV_WIDTH), center),
                pl.BlockSpec((None, hw, n_res * KV_WIDTH), right)]
    in_specs = ([pl.BlockSpec((None, tq, n_res * Q_WIDTH), center)] + kv_specs + kv_specs
                + [_resident(bias.shape)])
    args = [q, k, k, k, v, v, v, bias]
    if sink_rows is not None:
        in_specs.append(_resident(sink_rows.shape))
        args.append(sink_rows)
    out_shape = [jax.ShapeDtypeStruct(q.shape, BF16)]
    out_specs = [pl.BlockSpec((None, tq, n_res * Q_WIDTH), center)]
    if has_lse:
        out_shape.append(jax.ShapeDtypeStruct((bn, ls, dil * LANES), F32))
        out_specs.append(pl.BlockSpec((None, tq, n_res * LANES), center))
    win_rows = tq + 2 * hw
    return pl.pallas_call(
        functools.partial(_attn_kernel, tq=tq, hw=hw, n_res=n_res, n_blocks=ls // blk,
                          has_sink=sink_rows is not None, has_lse=has_lse),
        grid=(bn, dil // n_res, n_tiles),
        in_specs=in_specs,
        out_specs=out_specs,
        out_shape=out_shape,
        scratch_shapes=[pltpu.VMEM((win_rows, n_res * KV_WIDTH), BF16),
                        pltpu.VMEM((win_rows, n_res * KV_WIDTH), BF16)],
        compiler_params=pltpu.CompilerParams(dimension_semantics=("parallel", "parallel", "parallel"),
                                             vmem_limit_bytes=VMEM_LIMIT),
        name=f"attn_d{dil}",
    )(*args)


FFN_CHUNKS = ((0, 1536), (1536, 1280))


def _post_kernel(*refs, dils):
    n_groups = len(dils)
    o_refs, refs = refs[:n_groups], refs[n_groups:]
    if n_groups > 1:
        lse_refs, refs = refs[:n_groups], refs[n_groups:]
        e_ref, refs = refs[0], refs[1:]
    x_ref, wo_ref, g_ref, wgu_ref, wd_ref, out_ref = refs[:6]
    stages = dict(zip([g for g, dil in enumerate(dils) if dil > 1], refs[6:]))
    tm = x_ref.shape[0]

    def token_rows(ref, g, width, plane0):
        dil = dils[g]
        if dil == 1:
            return ref[...].astype(F32)
        stage = stages[g]
        n_planes = width // LANES
        for r in range(dil):
            for c in range(n_planes):
                col = r * width + c * LANES
                stage[plane0 + c, pl.ds(r, tm // dil, stride=dil), :] = ref[:, col:col + LANES].astype(F32)
        return jnp.concatenate([stage[plane0 + c] for c in range(n_planes)], axis=1)

    if n_groups == 1:
        o = o_refs[0][...]
    else:
        lses = [token_rows(lse_refs[g], g, LANES, Q_WIDTH // LANES) for g in range(n_groups)]
        mx = functools.reduce(jnp.maximum, lses)
        ex = [jnp.exp(l - mx) for l in lses]
        inv = 1.0 / functools.reduce(jnp.add, ex)
        o = None
        for g in range(n_groups):
            wt = ex[g] * inv
            hi = wt.astype(BF16)
            lo = (wt - hi.astype(F32)).astype(BF16)
            wexp = _dot(jnp.concatenate([hi, lo], axis=-1), e_ref[...])
            term = wexp * token_rows(o_refs[g], g, Q_WIDTH, 0)
            o = term if o is None else o + term
        o = o.astype(BF16)

    x1 = x_ref[...] + _dot(o, wo_ref[...])
    ms = jnp.mean(x1 * x1, axis=-1, keepdims=True)
    h = (x1 * lax.rsqrt(ms + EPS) * g_ref[...]).astype(BF16)
    acc = x1
    for c0, cw in FFN_CHUNKS:
        gate = _dot(h, wgu_ref[:, c0:c0 + cw])
        up = _dot(h, wgu_ref[:, D_FF + c0:D_FF + c0 + cw])
        act = (gate * jax.nn.sigmoid(gate) * up).astype(BF16)
        acc = acc + _dot(act, wd_ref[c0:c0 + cw, :])
    out_ref[...] = acc


def _post_call(os_, dils, lses, expand, x2d, wo, gain, wgu, wd, tm=512):
    m = x2d.shape[0]
    n_groups = len(os_)
    row = lambda width: pl.BlockSpec((tm, width), lambda i: (i, 0))
    in_specs = [pl.BlockSpec((tm // dil, dil * Q_WIDTH), lambda i: (i, 0)) for dil in dils]
    args = list(os_)
    if n_groups > 1:
        in_specs += [pl.BlockSpec((tm // dil, dil * LANES), lambda i: (i, 0)) for dil in dils]
        in_specs += [_resident(expand.shape)]
        args += list(lses) + [expand]
    in_specs += [row(D_MODEL), _resident(wo.shape), _resident(gain.shape), _resident(wgu.shape),
                 _resident(wd.shape)]
    args += [x2d, wo, gain, wgu, wd]
    return pl.pallas_call(
        functools.partial(_post_kernel, dils=tuple(dils)),
        grid=(m // tm,),
        in_specs=in_specs,
        out_specs=row(D_MODEL),
        out_shape=jax.ShapeDtypeStruct((m, D_MODEL), F32),
        scratch_shapes=[pltpu.VMEM((Q_WIDTH // LANES + 1, tm, LANES), F32) for dil in dils if dil > 1],
        compiler_params=pltpu.CompilerParams(dimension_semantics=("parallel",),
                                             vmem_limit_bytes=VMEM_LIMIT),
        name=f"post{n_groups}",
    )(*args)


def _t5_buckets(rel):
    nb = NUM_BUCKETS // 2
    max_exact = nb // 2
    n = np.abs(rel)
    large = max_exact + (np.log(np.maximum(n, 1) / max_exact)
                         / math.log(MAX_DISTANCE / max_exact) * (nb - max_exact)).astype(np.int32)
    large = np.minimum(large, nb - 1)
    return ((rel > 0).astype(np.int32) * nb + np.where(n < max_exact, n, large)).astype(np.int32)


def _bias_table(rel_table, hw, dil):
    blk, half = ATTN_BLK, ATTN_HALF
    win = half + 2 * hw
    rel = np.arange(win)[None, :] - hw - np.arange(half)[:, None]
    band = np.abs(rel) <= hw
    kj = np.arange(win)[None, :]
    masks = []
    for first in (False, True):
        for last in (False, True):
            per_half = []
            for h in range(2):
                ok = band.copy()
                if first:
                    ok &= kj >= hw - h * half
                if last:
                    ok &= kj < hw + blk - h * half
                per_half.append(ok)
            masks.append(np.stack(per_half))
    mask = jnp.asarray(np.stack(masks))[:, None, None, :, None]
    n = win + half - 1
    diff = np.arange(n)
    diff = np.where(diff < win, diff, diff - n) - hw
    w = jnp.take(rel_table.astype(F32), jnp.asarray(_t5_buckets(dil * diff)), axis=0).T
    tb = jnp.tile(w, (1, half))[:, :half * (n - 1)].reshape(N_HEADS, half, n - 1)[:, :, :win]
    tb = tb.reshape(1, 2, 2, 1, 4, half, win)
    bias = jnp.where(mask, tb, NEG)
    return bias.reshape(4, 2, 2, 2, 4 * half, win).swapaxes(-1, -2)


def _pair_cols(w):
    lead = w.shape[:-1]
    return jnp.swapaxes(w.reshape(*lead, 2, 2, 4, HEAD_DIM), -3, -2).reshape(*lead, Q_WIDTH)


def _qkv_weight(w, n_groups):
    parts = []
    for g in range(n_groups):
        base = g * QKV_WIDTH
        parts.append(_pair_cols(w[:, base:base + Q_WIDTH]))
        parts.append(w[:, base + Q_WIDTH:base + QKV_WIDTH])
    return jnp.concatenate(parts, axis=1).astype(BF16)


def _out_weight(w):
    return jnp.swapaxes(w.reshape(2, 2, 4, HEAD_DIM, D_MODEL), 1, 2).reshape(Q_WIDTH, D_MODEL).astype(BF16)


def _block_diag_ones():
    idx = np.arange(MXU_TILE) // HEAD_DIM
    return jnp.asarray((idx[:, None] == idx[None, :]).astype(np.float32), BF16)


def _expand_matrix():
    col = np.arange(Q_WIDTH)
    p, r, e = col // 512, (col % 512) // LANES, (col % LANES) // HEAD_DIM
    src = e * HEAD_DIM + p * 4 + r
    m = (np.arange(LANES)[:, None] == src[None, :]).astype(np.float32)
    return jnp.asarray(np.concatenate([m, m], axis=0), BF16)


def _trunk(x, p):
    bn, s, _ = x.shape
    m = bn * s
    x2d = x.reshape(m, D_MODEL)

    q, k, v = _qkv_call(x2d, p["norm_attn"][0], p["a_w"], p["bd"], p["a_qg"], p["a_kg"], (1,))
    (o,) = _attn_call(q.reshape(bn, s, Q_WIDTH), k.reshape(bn, s, KV_WIDTH), v.reshape(bn, s, KV_WIDTH),
                      p["a_bias"], p["a_sink"], 1, A_HALF_WINDOW, False)
    x2d = _post_call([o.reshape(m, Q_WIDTH)], (1,), None, None, x2d, p["a_wo"], p["norm_ffn"][0],
                     p["wgu"][0], p["wd"][0])

    dils = tuple(d for _, d in DILATED_GROUPS)
    outs = _qkv_call(x2d, p["norm_attn"][1], p["b_w"], p["bd"], p["b_qg"], p["b_kg"], dils)
    os_, lses = [], []
    for gi, (window, dil) in enumerate(DILATED_GROUPS):
        q, k, v = outs[3 * gi:3 * gi + 3]
        ls = s // dil
        o, lse = _attn_call(q.reshape(bn, ls, dil * Q_WIDTH), k.reshape(bn, ls, dil * KV_WIDTH),
                            v.reshape(bn, ls, dil * KV_WIDTH), p["b_bias"][gi], None, dil,
                            window // (2 * dil), True)
        os_.append(o.reshape(m // dil, dil * Q_WIDTH))
        lses.append(lse.reshape(m // dil, dil * LANES))
    x2d = _post_call(os_, dils, lses, p["expand"], x2d, p["b_wo"], p["norm_ffn"][1], p["wgu"][1], p["wd"][1])
    return x2d.reshape(bn, s, D_MODEL)


def kernel(x_prompt, x_sample, rel_table, norm_attn, norm_ffn, a_w_qkv, a_q_gain, a_k_gain, a_sink, a_w_o,
           b_w_qkv, b_q_gain, b_k_gain, b_w_o, ffn_w_gate_up, ffn_w_down):
    n_grp = len(DILATED_GROUPS)
    scale = HEAD_DIM ** -0.5
    half = ATTN_HALF
    sink_rows = jnp.broadcast_to(a_sink[0].astype(F32).reshape(2, 2, 1, 4, 1),
                                 (2, 2, 1, 4, half)).reshape(2, 2, 1, 4 * half)
    p = {
        "norm_attn": norm_attn.astype(F32).reshape(-1, 1, D_MODEL),
        "norm_ffn": norm_ffn.astype(F32).reshape(-1, 1, D_MODEL),
        "bd": _block_diag_ones(),
        "expand": _expand_matrix(),
        "a_w": _qkv_weight(a_w_qkv[0], 1),
        "a_qg": (jnp.tile(a_q_gain[0].astype(F32), N_HEADS) * scale).reshape(1, 1, Q_WIDTH),
        "a_kg": jnp.tile(a_k_gain[0].astype(F32), N_KV_HEADS).reshape(1, 1, KV_WIDTH),
        "a_bias": _bias_table(rel_table, A_HALF_WINDOW, 1),
        "a_sink": sink_rows,
        "a_wo": _out_weight(a_w_o[0]),
        "b_w": _qkv_weight(b_w_qkv[0], n_grp),
        "b_qg": (jnp.tile(b_q_gain[0].astype(F32), (1, N_HEADS)) * scale).reshape(n_grp, 1, Q_WIDTH),
        "b_kg": jnp.tile(b_k_gain[0].astype(F32), (1, N_KV_HEADS)).reshape(n_grp, 1, KV_WIDTH),
        "b_bias": [_bias_table(rel_table, w // (2 * d), d) for w, d in DILATED_GROUPS],
        "b_wo": _out_weight(b_w_o[0]),
        "wgu": ffn_w_gate_up.astype(BF16),
        "wd": ffn_w_down.astype(BF16),
    }
    return _trunk(x_prompt, p), _trunk(x_sample, p)
```

```python
import functools
import math

import numpy as np
import jax
import jax.numpy as jnp
from jax import lax
from jax.experimental import pallas as pl
from jax.experimental.pallas import tpu as pltpu

D_MODEL = 1024
N_HEADS = 16
N_KV_HEADS = 4
HEAD_DIM = 64
Q_WIDTH = N_HEADS * HEAD_DIM
KV_WIDTH = N_KV_HEADS * HEAD_DIM
QKV_WIDTH = Q_WIDTH + 2 * KV_WIDTH
D_FF = 2816
A_HALF_WINDOW = 128
DILATED_GROUPS = ((128, 1), (512, 4), (2048, 16))
NUM_BUCKETS = 32
MAX_DISTANCE = 1024
EPS = 1e-6
NEG = -1e30

LANES = 128
MXU_TILE = 256
ATTN_BLK = 128
ATTN_HALF = 64
ATTN_STEP_TOKENS = 1024
ATTN_OFFSETS = (2, 4)
DEINTERLEAVE_STRIDE = 4
VMEM_LIMIT = 56 * 1024 * 1024

BF16 = jnp.bfloat16
F32 = jnp.float32


def _dot(a, b):
    return jnp.dot(a, b, preferred_element_type=F32)


def _dot_nt(a, b):
    return lax.dot_general(a, b, (((1,), (1,)), ((), ())), preferred_element_type=F32)


def _dot_tn(a, b):
    return lax.dot_general(a, b, (((0,), (0,)), ((), ())), preferred_element_type=F32)


def _resident(shape):
    nd = len(shape)
    return pl.BlockSpec(shape, lambda *_: (0,) * nd, pipeline_mode=pl.Buffered(1))


def _qkv_kernel(x_ref, g_ref, w_ref, bd_ref, qg_ref, kg_ref, *refs, dils):
    out_refs, stages, stages2 = refs[:-2], refs[-2], refs[-1]
    tm = x_ref.shape[0]
    x = x_ref[...]
    ms = jnp.mean(x * x, axis=-1, keepdims=True)
    h = (x * lax.rsqrt(ms + EPS) * g_ref[...]).astype(BF16)
    bd = bd_ref[...]

    def emit(ref, dil, width, lo, val, slot):
        if dil == 1:
            ref[:, lo:lo + MXU_TILE] = val.astype(BF16)
            return
        stage, stage2 = stages.at[slot], stages2.at[slot]
        planes = range(MXU_TILE // LANES)
        for c in planes:
            stage[c] = val[:, c * LANES:(c + 1) * LANES]
        if dil <= DEINTERLEAVE_STRIDE:
            for r in range(dil):
                for c in planes:
                    col = r * width + lo + c * LANES
                    ref[:, col:col + LANES] = stage[c, pl.ds(r, tm // dil, stride=dil), :].astype(BF16)
            return
        s0, s1 = DEINTERLEAVE_STRIDE, dil // DEINTERLEAVE_STRIDE
        for r0 in range(s0):
            for c in planes:
                stage2[c, r0] = stage[c, pl.ds(r0, tm // s0, stride=s0), :]
        for r0 in range(s0):
            for r1 in range(s1):
                for c in planes:
                    col = (s0 * r1 + r0) * width + lo + c * LANES
                    ref[:, col:col + LANES] = stage2[c, r0, pl.ds(r1, tm // dil, stride=s1), :].astype(BF16)

    n_cols = QKV_WIDTH // MXU_TILE
    n_norm = (Q_WIDTH + KV_WIDTH) // MXU_TILE
    units = [(g, c) for g in reversed(range(len(dils))) for c in range(n_cols)]
    proj, sumsq = {}, {}

    def project(u):
        g, c = u
        lo = g * QKV_WIDTH + c * MXU_TILE
        proj[u] = _dot(h, w_ref[:, lo:lo + MXU_TILE])

    def square_sum(u):
        if u[1] < n_norm:
            t = proj[u]
            sumsq[u] = _dot((t * t).astype(BF16), bd)

    def write(i):
        g, c = u = units[i]
        q_ref, k_ref, v_ref = out_refs[3 * g:3 * g + 3]
        val = proj.pop(u)
        lo = c * MXU_TILE
        if c < n_norm:
            gain = qg_ref[g, :, lo:lo + MXU_TILE] if lo < Q_WIDTH else kg_ref[g]
            val = val * lax.rsqrt(sumsq.pop(u) + EPS) * gain
        ref, width, lo = ((q_ref, Q_WIDTH, lo) if lo < Q_WIDTH else
                          (k_ref, KV_WIDTH, 0) if c < n_norm else (v_ref, KV_WIDTH, 0))
        emit(ref, dils[g], width, lo, val, i % 2)

    n = len(units)
    for t in range(n + 2):
        if t < n:
            project(units[t])
        if 0 <= t - 1 < n:
            square_sum(units[t - 1])
        if 0 <= t - 2 < n:
            write(t - 2)


def _qkv_call(x2d, gain, w, bd, qg, kg, dils, tm=512):
    m = x2d.shape[0]
    out_shape, out_specs = [], []
    for dil in dils:
        for width in (Q_WIDTH, KV_WIDTH, KV_WIDTH):
            out_shape.append(jax.ShapeDtypeStruct((m // dil, dil * width), BF16))
            out_specs.append(pl.BlockSpec((tm // dil, dil * width), lambda i: (i, 0)))
    return pl.pallas_call(
        functools.partial(_qkv_kernel, dils=tuple(dils)),
        grid=(m // tm,),
        in_specs=[pl.BlockSpec((tm, D_MODEL), lambda i: (i, 0)), _resident(gain.shape), _resident(w.shape),
                  _resident(bd.shape), _resident(qg.shape), _resident(kg.shape)],
        out_specs=out_specs,
        out_shape=out_shape,
        scratch_shapes=[pltpu.VMEM((2, MXU_TILE // LANES, tm, LANES), F32),
                        pltpu.VMEM((2, MXU_TILE // LANES, DEINTERLEAVE_STRIDE, tm // DEINTERLEAVE_STRIDE, LANES),
                                   F32)],
        compiler_params=pltpu.CompilerParams(dimension_semantics=("parallel",),
                                             vmem_limit_bytes=VMEM_LIMIT),
        name=f"qkv{len(dils)}",
    )(x2d, gain, w, bd, qg, kg)


def _attn_kernel(*refs, tq, hw, n_res, n_blocks, has_sink, has_lse):
    q_ref, kl_ref, kc_ref, kr_ref, vl_ref, vc_ref, vr_ref, bias_ref = refs[:8]
    refs = refs[8:]
    sink_ref = None
    if has_sink:
        sink_ref, refs = refs[0], refs[1:]
    o_ref, refs = refs[0], refs[1:]
    lse_ref = None
    if has_lse:
        lse_ref, refs = refs[0], refs[1:]
    kw, vw = refs

    blk = ATTN_BLK
    half = ATTN_HALF
    win = half + 2 * hw

    for dst, (l, c, r) in ((kw, (kl_ref, kc_ref, kr_ref)), (vw, (vl_ref, vc_ref, vr_ref))):
        dst[0:hw, :] = l[...]
        dst[hw:hw + tq, :] = c[...]
        dst[hw + tq:hw + tq + hw, :] = r[...]

    tile = pl.program_id(2)
    n_inner = tq // blk
    lane = lax.broadcasted_iota(jnp.int32, (1, LANES), 1)
    keep = (lane < HEAD_DIM, lane >= HEAD_DIM)
    first_half = lax.broadcasted_iota(jnp.int32, (LANES, 1), 0) < HEAD_DIM
    sub = lax.broadcasted_iota(jnp.int32, (8, 1), 0)

    chunks = [(j, u, p, h, e) for u in range(n_res) for j in range(n_inner) for p in range(2)
              for h in range(2) for e in range(2)]
    kinds, lse_rows = {}, {}
    for j in range(n_inner):
        gb = tile * n_inner + j
        kinds[j] = 2 * (gb == 0).astype(jnp.int32) + (gb == n_blocks - 1).astype(jnp.int32)
    scores_, probs_, results_ = {}, {}, {}

    def scores(c):
        j, u, p, h, e = c
        r0 = j * blk + h * half
        kp = kw[r0:r0 + win, u * KV_WIDTH + p * LANES:u * KV_WIDTH + (p + 1) * LANES]
        base = u * Q_WIDTH + p * 512
        qcat = jnp.concatenate([q_ref[r0:r0 + half, base + r * LANES:base + (r + 1) * LANES]
                                for r in range(4)], axis=0)
        k_e = jnp.where(keep[e], kp, jnp.zeros((), BF16))
        scores_[c] = _dot_nt(k_e, qcat) + bias_ref[kinds[j], p, e, h]

    def probs(c):
        j, u, p, h, e = c
        st = scores_.pop(c)
        m = jnp.max(st, axis=0, keepdims=True)
        sk = None
        if has_sink:
            sk = sink_ref[p, e]
            m = jnp.maximum(m, sk)
        probs_[c] = (jnp.exp((st - m).astype(BF16)), m, sk)

    def values(c):
        j, u, p, h, e = c
        pt, m, sk = probs_.pop(c)
        r0 = j * blk + h * half
        vp = vw[r0:r0 + win, u * KV_WIDTH + p * LANES:u * KV_WIDTH + (p + 1) * LANES]
        v_e = jnp.where(keep[e], vp, jnp.ones((), BF16))
        ot = _dot_tn(v_e, pt)
        den = ot[HEAD_DIM:HEAD_DIM + 1] if e == 0 else ot[0:1]
        if has_sink:
            den = den + jnp.exp(sk - m)
        results_[c] = (ot, 1.0 / den, (m + jnp.log(den)) if has_lse else None)
        if e == 1:
            finish(j, u, p, h)

    def finish(j, u, p, h):
        (o0, i0, l0), (o1, i1, l1) = results_.pop((j, u, p, h, 0)), results_.pop((j, u, p, h, 1))
        zeros = jnp.zeros((8, blk), F32)
        lse_lo, lse_hi = lse_rows.get((j, u), (zeros, zeros))
        r0 = j * blk + h * half
        for rp in range(2):
            cs = slice(rp * LANES, (rp + 1) * LANES)
            t = jnp.where(first_half, o0[:, cs] * i0[:, cs], o1[:, cs] * i1[:, cs])
            tt = t.T.astype(BF16)
            for s in range(2):
                col = u * Q_WIDTH + p * 512 + (2 * rp + s) * LANES
                o_ref[r0:r0 + half, col:col + LANES] = tt[s * half:(s + 1) * half]
            if has_lse:
                v0, v1 = l0[:, cs], l1[:, cs]
                w0, w1 = pltpu.roll(v0, HEAD_DIM, axis=1), pltpu.roll(v1, HEAD_DIM, axis=1)
                for s in range(2):
                    mask = (sub == p * 4 + 2 * rp + s) & keep[h]
                    lse_lo = jnp.where(mask, v0 if s == h else w0, lse_lo)
                    lse_hi = jnp.where(mask, v1 if s == h else w1, lse_hi)
        lse_rows[(j, u)] = (lse_lo, lse_hi)
        if has_lse and p == 1 and h == 1:
            z = jnp.zeros((HEAD_DIM - 8, blk), F32)
            lse_ref[j * blk:(j + 1) * blk, u * LANES:(u + 1) * LANES] = (
                jnp.concatenate([lse_lo, z, lse_hi, z], axis=0).T)

    n = len(chunks)
    d1, d2 = ATTN_OFFSETS
    for t in range(n + d2):
        if t < n:
            scores(chunks[t])
        if 0 <= t - d1 < n:
            probs(chunks[t - d1])
        if 0 <= t - d2 < n:
            values(chunks[t - d2])


def _attn_call(q, k, v, bias, sink_rows, dil, hw, has_lse):
    bn, ls, _ = q.shape
    blk = ATTN_BLK
    tq = min(ATTN_STEP_TOKENS, ls)
    n_res = ATTN_STEP_TOKENS // tq
    assert ls % tq == 0 and tq % blk == 0 and tq % hw == 0 and blk % hw == 0 and dil % n_res == 0
    n_tiles = ls // tq
    ratio = tq // hw
    last_halo = ls // hw - 1

    center = lambda b, r, i: (b, i, r)
    left = lambda b, r, i: (b, jnp.maximum(i * ratio - 1, 0), r)
    right = lambda b, r, i: (b, jnp.minimum((i + 1) * ratio, last_halo), r)
    kv_specs = [pl.BlockSpec((None, hw, n_res * KV_WIDTH), left),
                pl.BlockSpec((None, tq, n_res * KV_WIDTH), center),
                pl.BlockSpec((None, hw, n_res * KV_WIDTH), right)]
    in_specs = ([pl.BlockSpec((None, tq, n_res * Q_WIDTH), center)] + kv_specs + kv_specs
                + [_resident(bias.shape)])
    args = [q, k, k, k, v, v, v, bias]
    if sink_rows is not None:
        in_specs.append(_resident(sink_rows.shape))
        args.append(sink_rows)
    out_shape = [jax.ShapeDtypeStruct(q.shape, BF16)]
    out_specs = [pl.BlockSpec((None, tq, n_res * Q_WIDTH), center)]
    if has_lse:
        out_shape.append(jax.ShapeDtypeStruct((bn, ls, dil * LANES), F32))
        out_specs.append(pl.BlockSpec((None, tq, n_res * LANES), center))
    win_rows = tq + 2 * hw
    return pl.pallas_call(
        functools.partial(_attn_kernel, tq=tq, hw=hw, n_res=n_res, n_blocks=ls // blk,
                          has_sink=sink_rows is not None, has_lse=has_lse),
        grid=(bn, dil // n_res, n_tiles),
        in_specs=in_specs,
        out_specs=out_specs,
        out_shape=out_shape,
        scratch_shapes=[pltpu.VMEM((win_rows, n_res * KV_WIDTH), BF16),
                        pltpu.VMEM((win_rows, n_res * KV_WIDTH), BF16)],
        compiler_params=pltpu.CompilerParams(dimension_semantics=("parallel", "parallel", "parallel"),
                                             vmem_limit_bytes=VMEM_LIMIT),
        name=f"attn_d{dil}",
    )(*args)


FFN_CHUNKS = ((0, 1536), (1536, 1280))


def _post_kernel(*refs, dils):
    n_groups = len(dils)
    o_refs, refs = refs[:n_groups], refs[n_groups:]
    if n_groups > 1:
        lse_refs, refs = refs[:n_groups], refs[n_groups:]
        e_ref, refs = refs[0], refs[1:]
    x_ref, wo_ref, g_ref, wgu_ref, wd_ref, out_ref = refs[:6]
    scratch = list(refs[6:])
    stages, stages2 = {}, {}
    for g, dil in enumerate(dils):
        if dil > 1:
            stages[g] = scratch.pop(0)
        if dil > DEINTERLEAVE_STRIDE:
            stages2[g] = scratch.pop(0)
    tm = x_ref.shape[0]

    def token_rows(ref, g, width, plane0):
        dil = dils[g]
        if dil == 1:
            return ref[...].astype(F32)
        stage = stages[g]
        planes = [plane0 + c for c in range(width // LANES)]
        piece = lambda r, c: ref[:, r * width + c * LANES:r * width + (c + 1) * LANES].astype(F32)
        if dil <= DEINTERLEAVE_STRIDE:
            for r in range(dil):
                for c, pc in enumerate(planes):
                    stage[pc, pl.ds(r, tm // dil, stride=dil), :] = piece(r, c)
        else:
            stage2 = stages2[g]
            s0, s1 = DEINTERLEAVE_STRIDE, dil // DEINTERLEAVE_STRIDE
            for r0 in range(s0):
                for r1 in range(s1):
                    for c, pc in enumerate(planes):
                        stage2[pc, r0, pl.ds(r1, tm // dil, stride=s1), :] = piece(s0 * r1 + r0, c)
            for r0 in range(s0):
                for pc in planes:
                    stage[pc, pl.ds(r0, tm // s0, stride=s0), :] = stage2[pc, r0]
        return jnp.concatenate([stage[pc] for pc in planes], axis=1)

    if n_groups == 1:
        o = o_refs[0][...]
    else:
        lses = [token_rows(lse_refs[g], g, LANES, Q_WIDTH // LANES) for g in range(n_groups)]
        mx = functools.reduce(jnp.maximum, lses)
        ex = [jnp.exp(l - mx) for l in lses]
        inv = 1.0 / functools.reduce(jnp.add, ex)
        o = None
        for g in range(n_groups):
            wt = ex[g] * inv
            hi = wt.astype(BF16)
            lo = (wt - hi.astype(F32)).astype(BF16)
            wexp = _dot(jnp.concatenate([hi, lo], axis=-1), e_ref[...])
            term = wexp * token_rows(o_refs[g], g, Q_WIDTH, 0)
            o = term if o is None else o + term
        o = o.astype(BF16)

    x1 = x_ref[...] + _dot(o, wo_ref[...])
    ms = jnp.mean(x1 * x1, axis=-1, keepdims=True)
    h = (x1 * lax.rsqrt(ms + EPS) * g_ref[...]).astype(BF16)
    acc = x1
    for c0, cw in FFN_CHUNKS:
        gate = _dot(h, wgu_ref[:, c0:c0 + cw])
        up = _dot(h, wgu_ref[:, D_FF + c0:D_FF + c0 + cw])
        act = (gate * jax.nn.sigmoid(gate) * up).astype(BF16)
        acc = acc + _dot(act, wd_ref[c0:c0 + cw, :])
    out_ref[...] = acc


def _post_call(os_, dils, lses, expand, x2d, wo, gain, wgu, wd, tm=512):
    m = x2d.shape[0]
    n_groups = len(os_)
    row = lambda width: pl.BlockSpec((tm, width), lambda i: (i, 0))
    in_specs = [pl.BlockSpec((tm // dil, dil * Q_WIDTH), lambda i: (i, 0)) for dil in dils]
    args = list(os_)
    if n_groups > 1:
        in_specs += [pl.BlockSpec((tm // dil, dil * LANES), lambda i: (i, 0)) for dil in dils]
        in_specs += [_resident(expand.shape)]
        args += list(lses) + [expand]
    in_specs += [row(D_MODEL), _resident(wo.shape), _resident(gain.shape), _resident(wgu.shape),
                 _resident(wd.shape)]
    args += [x2d, wo, gain, wgu, wd]
    n_planes = Q_WIDTH // LANES + 1
    scratch = []
    for dil in dils:
        if dil > 1:
            scratch.append(pltpu.VMEM((n_planes, tm, LANES), F32))
        if dil > DEINTERLEAVE_STRIDE:
            scratch.append(pltpu.VMEM((n_planes, DEINTERLEAVE_STRIDE, tm // DEINTERLEAVE_STRIDE, LANES), F32))
    return pl.pallas_call(
        functools.partial(_post_kernel, dils=tuple(dils)),
        grid=(m // tm,),
        in_specs=in_specs,
        out_specs=row(D_MODEL),
        out_shape=jax.ShapeDtypeStruct((m, D_MODEL), F32),
        scratch_shapes=scratch,
        compiler_params=pltpu.CompilerParams(dimension_semantics=("parallel",),
                                             vmem_limit_bytes=VMEM_LIMIT),
        name=f"post{n_groups}",
    )(*args)


def _t5_buckets(rel):
    nb = NUM_BUCKETS // 2
    max_exact = nb // 2
    n = np.abs(rel)
    large = max_exact + (np.log(np.maximum(n, 1) / max_exact)
                         / math.log(MAX_DISTANCE / max_exact) * (nb - max_exact)).astype(np.int32)
    large = np.minimum(large, nb - 1)
    return ((rel > 0).astype(np.int32) * nb + np.where(n < max_exact, n, large)).astype(np.int32)


def _bias_table(rel_table, hw, dil):
    blk, half = ATTN_BLK, ATTN_HALF
    win = half + 2 * hw
    rel = np.arange(win)[None, :] - hw - np.arange(half)[:, None]
    band = np.abs(rel) <= hw
    kj = np.arange(win)[None, :]
    masks = []
    for first in (False, True):
        for last in (False, True):
            per_half = []
            for h in range(2):
                ok = band.copy()
                if first:
                    ok &= kj >= hw - h * half
                if last:
                    ok &= kj < hw + blk - h * half
                per_half.append(ok)
            masks.append(np.stack(per_half))
    mask = jnp.asarray(np.stack(masks))[:, None, None, :, None]
    n = win + half - 1
    diff = np.arange(n)
    diff = np.where(diff < win, diff, diff - n) - hw
    w = jnp.take(rel_table.astype(F32), jnp.asarray(_t5_buckets(dil * diff)), axis=0).T
    tb = jnp.tile(w, (1, half))[:, :half * (n - 1)].reshape(N_HEADS, half, n - 1)[:, :, :win]
    tb = tb.reshape(1, 2, 2, 1, 4, half, win)
    bias = jnp.where(mask, tb, NEG)
    return bias.reshape(4, 2, 2, 2, 4 * half, win).swapaxes(-1, -2)


def _pair_cols(w):
    lead = w.shape[:-1]
    return jnp.swapaxes(w.reshape(*lead, 2, 2, 4, HEAD_DIM), -3, -2).reshape(*lead, Q_WIDTH)


def _qkv_weight(w, n_groups):
    parts = []
    for g in range(n_groups):
        base = g * QKV_WIDTH
        parts.append(_pair_cols(w[:, base:base + Q_WIDTH]))
        parts.append(w[:, base + Q_WIDTH:base + QKV_WIDTH])
    return jnp.concatenate(parts, axis=1).astype(BF16)


def _out_weight(w):
    return jnp.swapaxes(w.reshape(2, 2, 4, HEAD_DIM, D_MODEL), 1, 2).reshape(Q_WIDTH, D_MODEL).astype(BF16)


def _block_diag_mean():
    idx = np.arange(MXU_TILE) // HEAD_DIM
    return jnp.asarray((idx[:, None] == idx[None, :]).astype(np.float32) / HEAD_DIM, BF16)


def _expand_matrix():
    col = np.arange(Q_WIDTH)
    p, r, e = col // 512, (col % 512) // LANES, (col % LANES) // HEAD_DIM
    src = e * HEAD_DIM + p * 4 + r
    m = (np.arange(LANES)[:, None] == src[None, :]).astype(np.float32)
    return jnp.asarray(np.concatenate([m, m], axis=0), BF16)


def _trunk(x, p):
    bn, s, _ = x.shape
    m = bn * s
    x2d = x.reshape(m, D_MODEL)

    q, k, v = _qkv_call(x2d, p["norm_attn"][0], p["a_w"], p["bd"], p["a_qg"], p["a_kg"], (1,))
    (o,) = _attn_call(q.reshape(bn, s, Q_WIDTH), k.reshape(bn, s, KV_WIDTH), v.reshape(bn, s, KV_WIDTH),
                      p["a_bias"], p["a_sink"], 1, A_HALF_WINDOW, False)
    x2d = _post_call([o.reshape(m, Q_WIDTH)], (1,), None, None, x2d, p["a_wo"], p["norm_ffn"][0],
                     p["wgu"][0], p["wd"][0])

    dils = tuple(d for _, d in DILATED_GROUPS)
    outs = _qkv_call(x2d, p["norm_attn"][1], p["b_w"], p["bd"], p["b_qg"], p["b_kg"], dils)
    os_, lses = [], []
    for gi, (window, dil) in enumerate(DILATED_GROUPS):
        q, k, v = outs[3 * gi:3 * gi + 3]
        ls = s // dil
        o, lse = _attn_call(q.reshape(bn, ls, dil * Q_WIDTH), k.reshape(bn, ls, dil * KV_WIDTH),
                            v.reshape(bn, ls, dil * KV_WIDTH), p["b_bias"][gi], None, dil,
                            window // (2 * dil), True)
        os_.append(o.reshape(m // dil, dil * Q_WIDTH))
        lses.append(lse.reshape(m // dil, dil * LANES))
    x2d = _post_call(os_, dils, lses, p["expand"], x2d, p["b_wo"], p["norm_ffn"][1], p["wgu"][1], p["wd"][1])
    return x2d.reshape(bn, s, D_MODEL)


def kernel(x_prompt, x_sample, rel_table, norm_attn, norm_ffn, a_w_qkv, a_q_gain, a_k_gain, a_sink, a_w_o,
           b_w_qkv, b_q_gain, b_k_gain, b_w_o, ffn_w_gate_up, ffn_w_down):
    n_grp = len(DILATED_GROUPS)
    scale = HEAD_DIM ** -0.5
    half = ATTN_HALF
    sink_rows = jnp.broadcast_to(a_sink[0].astype(F32).reshape(2, 2, 1, 4, 1),
                                 (2, 2, 1, 4, half)).reshape(2, 2, 1, 4 * half)
    p = {
        "norm_attn": norm_attn.astype(F32).reshape(-1, 1, D_MODEL),
        "norm_ffn": norm_ffn.astype(F32).reshape(-1, 1, D_MODEL),
        "bd": _block_diag_mean(),
        "expand": _expand_matrix(),
        "a_w": _qkv_weight(a_w_qkv[0], 1),
        "a_qg": (jnp.tile(a_q_gain[0].astype(F32), N_HEADS) * scale).reshape(1, 1, Q_WIDTH),
        "a_kg": jnp.tile(a_k_gain[0].astype(F32), N_KV_HEADS).reshape(1, 1, KV_WIDTH),
        "a_bias": _bias_table(rel_table, A_HALF_WINDOW, 1),
        "a_sink": sink_rows,
        "a_wo": _out_weight(a_w_o[0]),
        "b_w": _qkv_weight(b_w_qkv[0], n_grp),
        "b_qg": (jnp.tile(b_q_gain[0].astype(F32), (1, N_HEADS)) * scale).reshape(n_grp, 1, Q_WIDTH),
        "b_kg": jnp.tile(b_k_gain[0].astype(F32), (1, N_KV_HEADS)).reshape(n_grp, 1, KV_WIDTH),
        "b_bias": [_bias_table(rel_table, w // (2 * d), d) for w, d in DILATED_GROUPS],
        "b_wo": _out_weight(b_w_o[0]),
        "wgu": ffn_w_gate_up.astype(BF16),
        "wd": ffn_w_down.astype(BF16),
    }
    return _trunk(x_prompt, p), _trunk(x_sample, p)
```

```python
import functools
import math

import numpy as np
import jax
import jax.numpy as jnp
from jax import lax
from jax.experimental import pallas as pl
from jax.experimental.pallas import tpu as pltpu

D_MODEL = 1024
N_HEADS = 16
N_KV_HEADS = 4
HEAD_DIM = 64
Q_WIDTH = N_HEADS * HEAD_DIM
KV_WIDTH = N_KV_HEADS * HEAD_DIM
QKV_WIDTH = Q_WIDTH + 2 * KV_WIDTH
D_FF = 2816
A_HALF_WINDOW = 128
DILATED_GROUPS = ((128, 1), (512, 4), (2048, 16))
NUM_BUCKETS = 32
MAX_DISTANCE = 1024
EPS = 1e-6
NEG = -1e30

PAIR_WIDTH = Q_WIDTH // 2
LANES = 128
MXU_TILE = 256
ATTN_BLK = 128
ATTN_HALF = 64
ATTN_STEP_TOKENS = 2048
ATTN_OFFSETS = (2, 4)
DEINTERLEAVE_STRIDE = 4
VMEM_LIMIT = 56 * 1024 * 1024

BF16 = jnp.bfloat16
F32 = jnp.float32


def _dot(a, b):
    return jnp.dot(a, b, preferred_element_type=F32)


def _dot_nt(a, b):
    return lax.dot_general(a, b, (((1,), (1,)), ((), ())), preferred_element_type=F32)


def _dot_tn(a, b):
    return lax.dot_general(a, b, (((0,), (0,)), ((), ())), preferred_element_type=F32)


def _resident(shape):
    nd = len(shape)
    return pl.BlockSpec(shape, lambda *_: (0,) * nd, pipeline_mode=pl.Buffered(1))


def _qkv_kernel(x_ref, g_ref, w_ref, bd_ref, qg_ref, kg_ref, *refs, dils):
    out_refs, stages, stages2 = refs[:-2], refs[-2], refs[-1]
    tm = x_ref.shape[0]
    x = x_ref[...]
    ms = jnp.mean(x * x, axis=-1, keepdims=True)
    h = (x * lax.rsqrt(ms + EPS) * g_ref[...]).astype(BF16)
    bd = bd_ref[...]

    def emit(ref, dil, width, lo, val, slot):
        if dil == 1:
            ref[:, lo:lo + MXU_TILE] = val.astype(BF16)
            return
        stage, stage2 = stages.at[slot], stages2.at[slot]
        planes = range(MXU_TILE // LANES)
        for c in planes:
            stage[c] = val[:, c * LANES:(c + 1) * LANES]
        if dil <= DEINTERLEAVE_STRIDE:
            for r in range(dil):
                for c in planes:
                    col = r * width + lo + c * LANES
                    ref[:, col:col + LANES] = stage[c, pl.ds(r, tm // dil, stride=dil), :].astype(BF16)
            return
        s0, s1 = DEINTERLEAVE_STRIDE, dil // DEINTERLEAVE_STRIDE
        for r0 in range(s0):
            for c in planes:
                stage2[c, r0] = stage[c, pl.ds(r0, tm // s0, stride=s0), :]
        for r0 in range(s0):
            for r1 in range(s1):
                for c in planes:
                    col = (s0 * r1 + r0) * width + lo + c * LANES
                    ref[:, col:col + LANES] = stage2[c, r0, pl.ds(r1, tm // dil, stride=s1), :].astype(BF16)

    n_cols = QKV_WIDTH // MXU_TILE
    n_norm = (Q_WIDTH + KV_WIDTH) // MXU_TILE
    units = [(g, c) for g in reversed(range(len(dils))) for c in range(n_cols)]
    proj, sumsq = {}, {}

    def project(u):
        g, c = u
        lo = g * QKV_WIDTH + c * MXU_TILE
        proj[u] = _dot(h, w_ref[:, lo:lo + MXU_TILE])

    def square_sum(u):
        if u[1] < n_norm:
            t = proj[u]
            sumsq[u] = _dot((t * t).astype(BF16), bd)

    def write(i):
        g, c = u = units[i]
        q_ref, k_ref, v_ref = out_refs[3 * g:3 * g + 3]
        val = proj.pop(u)
        lo = c * MXU_TILE
        if c < n_norm:
            gain = qg_ref[g, :, lo:lo + MXU_TILE] if lo < Q_WIDTH else kg_ref[g]
            val = val * lax.rsqrt(sumsq.pop(u) + EPS) * gain
        ref, width, lo = ((q_ref, Q_WIDTH, lo) if lo < Q_WIDTH else
                          (k_ref, KV_WIDTH, 0) if c < n_norm else (v_ref, KV_WIDTH, 0))
        emit(ref, dils[g], width, lo, val, i % 2)

    n = len(units)
    for t in range(n + 2):
        if t < n:
            project(units[t])
        if 0 <= t - 1 < n:
            square_sum(units[t - 1])
        if 0 <= t - 2 < n:
            write(t - 2)


def _qkv_call(x2d, gain, w, bd, qg, kg, dils, tm=1024):
    m = x2d.shape[0]
    out_shape, out_specs = [], []
    for dil in dils:
        for width in (Q_WIDTH, KV_WIDTH, KV_WIDTH):
            out_shape.append(jax.ShapeDtypeStruct((m // dil, dil * width), BF16))
            out_specs.append(pl.BlockSpec((tm // dil, dil * width), lambda i: (i, 0)))
    return pl.pallas_call(
        functools.partial(_qkv_kernel, dils=tuple(dils)),
        grid=(m // tm,),
        in_specs=[pl.BlockSpec((tm, D_MODEL), lambda i: (i, 0)), _resident(gain.shape), _resident(w.shape),
                  _resident(bd.shape), _resident(qg.shape), _resident(kg.shape)],
        out_specs=out_specs,
        out_shape=out_shape,
        scratch_shapes=[pltpu.VMEM((2, MXU_TILE // LANES, tm, LANES), F32),
                        pltpu.VMEM((2, MXU_TILE // LANES, DEINTERLEAVE_STRIDE, tm // DEINTERLEAVE_STRIDE, LANES),
                                   F32)],
        compiler_params=pltpu.CompilerParams(dimension_semantics=("parallel",),
                                             vmem_limit_bytes=VMEM_LIMIT),
        name=f"qkv{len(dils)}",
    )(x2d, gain, w, bd, qg, kg)


def _attn_kernel(*refs, tq, hw, n_res, n_blocks, has_sink, has_lse):
    q_ref, kl_ref, kc_ref, kr_ref, vl_ref, vc_ref, vr_ref, bias_ref = refs[:8]
    refs = refs[8:]
    sink_ref = None
    if has_sink:
        sink_ref, refs = refs[0], refs[1:]
    o_ref, refs = refs[0], refs[1:]
    lse_ref = None
    if has_lse:
        lse_ref, refs = refs[0], refs[1:]
    kw, vw = refs

    blk = ATTN_BLK
    half = ATTN_HALF
    win = half + 2 * hw

    for dst, (l, c, r) in ((kw, (kl_ref, kc_ref, kr_ref)), (vw, (vl_ref, vc_ref, vr_ref))):
        dst[0:hw, :] = l[...]
        dst[hw:hw + tq, :] = c[...]
        dst[hw + tq:hw + tq + hw, :] = r[...]

    tile = pl.program_id(2)
    n_inner = tq // blk
    lane = lax.broadcasted_iota(jnp.int32, (1, LANES), 1)
    keep = (lane < HEAD_DIM, lane >= HEAD_DIM)
    first_half = lax.broadcasted_iota(jnp.int32, (LANES, 1), 0) < HEAD_DIM
    sub = lax.broadcasted_iota(jnp.int32, (8, 1), 0)

    chunks = [(j, u, p, h, e) for u in range(n_res) for j in range(n_inner) for p in range(2)
              for h in range(2) for e in range(2)]
    kinds, lse_rows = {}, {}
    for j in range(n_inner):
        gb = tile * n_inner + j
        kinds[j] = 2 * (gb == 0).astype(jnp.int32) + (gb == n_blocks - 1).astype(jnp.int32)
    scores_, probs_, results_ = {}, {}, {}

    def scores(c):
        j, u, p, h, e = c
        r0 = j * blk + h * half
        kp = kw[r0:r0 + win, u * KV_WIDTH + p * LANES:u * KV_WIDTH + (p + 1) * LANES]
        base = u * Q_WIDTH + p * PAIR_WIDTH
        qcat = jnp.concatenate([q_ref[r0:r0 + half, base + r * LANES:base + (r + 1) * LANES]
                                for r in range(4)], axis=0)
        k_e = jnp.where(keep[e], kp, jnp.zeros((), BF16))
        scores_[c] = _dot_nt(k_e, qcat) + bias_ref[kinds[j], p, e, h]

    def probs(c):
        j, u, p, h, e = c
        st = scores_.pop(c)
        m = jnp.max(st, axis=0, keepdims=True)
        sk = None
        if has_sink:
            sk = sink_ref[p, e]
            m = jnp.maximum(m, sk)
        probs_[c] = (jnp.exp((st - m).astype(BF16)), m, sk)

    def values(c):
        j, u, p, h, e = c
        pt, m, sk = probs_.pop(c)
        r0 = j * blk + h * half
        vp = vw[r0:r0 + win, u * KV_WIDTH + p * LANES:u * KV_WIDTH + (p + 1) * LANES]
        v_e = jnp.where(keep[e], vp, jnp.ones((), BF16))
        ot = _dot_tn(v_e, pt)
        den = ot[HEAD_DIM:HEAD_DIM + 1] if e == 0 else ot[0:1]
        if has_sink:
            den = den + jnp.exp(sk - m)
        results_[c] = (ot, 1.0 / den, (m + jnp.log(den)) if has_lse else None)
        if e == 1:
            finish(j, u, p, h)

    def finish(j, u, p, h):
        (o0, i0, l0), (o1, i1, l1) = results_.pop((j, u, p, h, 0)), results_.pop((j, u, p, h, 1))
        zeros = jnp.zeros((8, blk), F32)
        lse_lo, lse_hi = lse_rows.get((j, u), (zeros, zeros))
        r0 = j * blk + h * half
        for rp in range(2):
            cs = slice(rp * LANES, (rp + 1) * LANES)
            t = jnp.where(first_half, o0[:, cs] * i0[:, cs], o1[:, cs] * i1[:, cs])
            tt = t.T.astype(BF16)
            for s in range(2):
                col = u * Q_WIDTH + p * PAIR_WIDTH + (2 * rp + s) * LANES
                o_ref[r0:r0 + half, col:col + LANES] = tt[s * half:(s + 1) * half]
            if has_lse:
                v0, v1 = l0[:, cs], l1[:, cs]
                w0, w1 = pltpu.roll(v0, HEAD_DIM, axis=1), pltpu.roll(v1, HEAD_DIM, axis=1)
                for s in range(2):
                    mask = (sub == p * 4 + 2 * rp + s) & keep[h]
                    lse_lo = jnp.where(mask, v0 if s == h else w0, lse_lo)
                    lse_hi = jnp.where(mask, v1 if s == h else w1, lse_hi)
        lse_rows[(j, u)] = (lse_lo, lse_hi)
        if has_lse and p == 1 and h == 1:
            z = jnp.zeros((HEAD_DIM - 8, blk), F32)
            lse_ref[j * blk:(j + 1) * blk, u * LANES:(u + 1) * LANES] = (
                jnp.concatenate([lse_lo, z, lse_hi, z], axis=0).T)

    n = len(chunks)
    d1, d2 = ATTN_OFFSETS
    for t in range(n + d2):
        if t < n:
            scores(chunks[t])
        if 0 <= t - d1 < n:
            probs(chunks[t - d1])
        if 0 <= t - d2 < n:
            values(chunks[t - d2])


def _attn_call(q, k, v, bias, sink_rows, dil, hw, has_lse):
    bn, ls, _ = q.shape
    blk = ATTN_BLK
    tq = min(ATTN_STEP_TOKENS, ls)
    n_res = ATTN_STEP_TOKENS // tq
    assert ls % tq == 0 and tq % blk == 0 and tq % hw == 0 and blk % hw == 0 and dil % n_res == 0
    n_tiles = ls // tq
    ratio = tq // hw
    last_halo = ls // hw - 1

    center = lambda b, r, i: (b, i, r)
    left = lambda b, r, i: (b, jnp.maximum(i * ratio - 1, 0), r)
    right = lambda b, r, i: (b, jnp.minimum((i + 1) * ratio, last_halo), r)
    kv_specs = [pl.BlockSpec((None, hw, n_res * KV_WIDTH), left),
                pl.BlockSpec((None, tq, n_res * KV_WIDTH), center),
                pl.BlockSpec((None, hw, n_res * KV_WIDTH), right)]
    in_specs = ([pl.BlockSpec((None, tq, n_res * Q_WIDTH), center)] + kv_specs + kv_specs
                + [_resident(bias.shape)])
    args = [q, k, k, k, v, v, v, bias]
    if sink_rows is not None:
        in_specs.append(_resident(sink_rows.shape))
        args.append(sink_rows)
    out_shape = [jax.ShapeDtypeStruct(q.shape, BF16)]
    out_specs = [pl.BlockSpec((None, tq, n_res * Q_WIDTH), center)]
    if has_lse:
        out_shape.append(jax.ShapeDtypeStruct((bn, ls, dil * LANES), F32))
        out_specs.append(pl.BlockSpec((None, tq, n_res * LANES), center))
    win_rows = tq + 2 * hw
    return pl.pallas_call(
        functools.partial(_attn_kernel, tq=tq, hw=hw, n_res=n_res, n_blocks=ls // blk,
                          has_sink=sink_rows is not None, has_lse=has_lse),
        grid=(bn, dil // n_res, n_tiles),
        in_specs=in_specs,
        out_specs=out_specs,
        out_shape=out_shape,
        scratch_shapes=[pltpu.VMEM((win_rows, n_res * KV_WIDTH), BF16),
                        pltpu.VMEM((win_rows, n_res * KV_WIDTH), BF16)],
        compiler_params=pltpu.CompilerParams(dimension_semantics=("parallel", "parallel", "parallel"),
                                             vmem_limit_bytes=VMEM_LIMIT),
        name=f"attn_d{dil}",
    )(*args)


FFN_CHUNKS = ((0, 1536), (1536, 1280))


def _post_kernel(*refs, dils):
    n_groups = len(dils)
    o_refs, refs = refs[:n_groups], refs[n_groups:]
    if n_groups > 1:
        lse_refs, refs = refs[:n_groups], refs[n_groups:]
        e_ref, refs = refs[0], refs[1:]
    x_ref, wo_ref, g_ref, wgu_ref, wd_ref, out_ref = refs[:6]
    scratch = list(refs[6:])
    stages, stages2 = {}, {}
    for g, dil in enumerate(dils):
        if dil > 1:
            stages[g] = scratch.pop(0)
        if dil > DEINTERLEAVE_STRIDE:
            stages2[g] = scratch.pop(0)
    tm = x_ref.shape[0]

    def token_rows(ref, g, width, plane0):
        dil = dils[g]
        if dil == 1:
            return ref[...].astype(F32)
        stage = stages[g]
        planes = [plane0 + c for c in range(width // LANES)]
        piece = lambda r, c: ref[:, r * width + c * LANES:r * width + (c + 1) * LANES].astype(F32)
        if dil <= DEINTERLEAVE_STRIDE:
            for r in range(dil):
                for c, pc in enumerate(planes):
                    stage[pc, pl.ds(r, tm // dil, stride=dil), :] = piece(r, c)
        else:
            stage2 = stages2[g]
            s0, s1 = DEINTERLEAVE_STRIDE, dil // DEINTERLEAVE_STRIDE
            for r0 in range(s0):
                for r1 in range(s1):
                    for c, pc in enumerate(planes):
                        stage2[pc, r0, pl.ds(r1, tm // dil, stride=s1), :] = piece(s0 * r1 + r0, c)
            for r0 in range(s0):
                for pc in planes:
                    stage[pc, pl.ds(r0, tm // s0, stride=s0), :] = stage2[pc, r0]
        return jnp.concatenate([stage[pc] for pc in planes], axis=1)

    if n_groups == 1:
        o = o_refs[0][...]
    else:
        lses = [token_rows(lse_refs[g], g, LANES, Q_WIDTH // LANES) for g in range(n_groups)]
        mx = functools.reduce(jnp.maximum, lses)
        ex = [jnp.exp(l - mx) for l in lses]
        inv = 1.0 / functools.reduce(jnp.add, ex)
        last = token_rows(o_refs[n_groups - 1], n_groups - 1, Q_WIDTH, 0)
        o = last
        for g in range(n_groups - 1):
            wt = ex[g] * inv
            hi = wt.astype(BF16)
            lo = (wt - hi.astype(F32)).astype(BF16)
            wexp = _dot(jnp.concatenate([hi, lo], axis=-1), e_ref[...])
            o = o + wexp * (token_rows(o_refs[g], g, Q_WIDTH, 0) - last)
        o = o.astype(BF16)

    x1 = x_ref[...] + _dot(o, wo_ref[...])
    ms = jnp.mean(x1 * x1, axis=-1, keepdims=True)
    h = (x1 * lax.rsqrt(ms + EPS) * g_ref[...]).astype(BF16)
    acc = x1
    for c0, cw in FFN_CHUNKS:
        gate = _dot(h, wgu_ref[:, c0:c0 + cw])
        up = _dot(h, wgu_ref[:, D_FF + c0:D_FF + c0 + cw])
        act = (gate * jax.nn.sigmoid(gate) * up).astype(BF16)
        acc = acc + _dot(act, wd_ref[c0:c0 + cw, :])
    out_ref[...] = acc


def _post_call(os_, dils, lses, expand, x2d, wo, gain, wgu, wd, tm=512):
    m = x2d.shape[0]
    n_groups = len(os_)
    row = lambda width: pl.BlockSpec((tm, width), lambda i: (i, 0))
    in_specs = [pl.BlockSpec((tm // dil, dil * Q_WIDTH), lambda i: (i, 0)) for dil in dils]
    args = list(os_)
    if n_groups > 1:
        in_specs += [pl.BlockSpec((tm // dil, dil * LANES), lambda i: (i, 0)) for dil in dils]
        in_specs += [_resident(expand.shape)]
        args += list(lses) + [expand]
    in_specs += [row(D_MODEL), _resident(wo.shape), _resident(gain.shape), _resident(wgu.shape),
                 _resident(wd.shape)]
    args += [x2d, wo, gain, wgu, wd]
    n_planes = Q_WIDTH // LANES + 1
    scratch = []
    for dil in dils:
        if dil > 1:
            scratch.append(pltpu.VMEM((n_planes, tm, LANES), F32))
        if dil > DEINTERLEAVE_STRIDE:
            scratch.append(pltpu.VMEM((n_planes, DEINTERLEAVE_STRIDE, tm // DEINTERLEAVE_STRIDE, LANES), F32))
    return pl.pallas_call(
        functools.partial(_post_kernel, dils=tuple(dils)),
        grid=(m // tm,),
        in_specs=in_specs,
        out_specs=row(D_MODEL),
        out_shape=jax.ShapeDtypeStruct((m, D_MODEL), F32),
        scratch_shapes=scratch,
        compiler_params=pltpu.CompilerParams(dimension_semantics=("parallel",),
                                             vmem_limit_bytes=VMEM_LIMIT),
        name=f"post{n_groups}",
    )(*args)


def _t5_buckets(rel):
    nb = NUM_BUCKETS // 2
    max_exact = nb // 2
    n = np.abs(rel)
    large = max_exact + (np.log(np.maximum(n, 1) / max_exact)
                         / math.log(MAX_DISTANCE / max_exact) * (nb - max_exact)).astype(np.int32)
    large = np.minimum(large, nb - 1)
    return ((rel > 0).astype(np.int32) * nb + np.where(n < max_exact, n, large)).astype(np.int32)


def _bias_table(rel_table, hw, dil):
    blk, half = ATTN_BLK, ATTN_HALF
    win = half + 2 * hw
    rel = np.arange(win)[None, :] - hw - np.arange(half)[:, None]
    band = np.abs(rel) <= hw
    kj = np.arange(win)[None, :]
    masks = []
    for first in (False, True):
        for last in (False, True):
            per_half = []
            for h in range(2):
                ok = band.copy()
                if first:
                    ok &= kj >= hw - h * half
                if last:
                    ok &= kj < hw + blk - h * half
                per_half.append(ok)
            masks.append(np.stack(per_half))
    mask = jnp.asarray(np.stack(masks))[:, None, None, :, None]
    n = win + half - 1
    diff = np.arange(n)
    diff = np.where(diff < win, diff, diff - n) - hw
    w = jnp.take(rel_table.astype(F32), jnp.asarray(_t5_buckets(dil * diff)), axis=0).T
    tb = jnp.tile(w, (1, half))[:, :half * (n - 1)].reshape(N_HEADS, half, n - 1)[:, :, :win]
    tb = tb.reshape(1, 2, 2, 1, 4, half, win)
    bias = jnp.where(mask, tb, NEG)
    return bias.reshape(4, 2, 2, 2, 4 * half, win).swapaxes(-1, -2)


def _pair_cols(w):
    lead = w.shape[:-1]
    return jnp.swapaxes(w.reshape(*lead, 2, 2, 4, HEAD_DIM), -3, -2).reshape(*lead, Q_WIDTH)


def _qkv_weight(w, n_groups):
    parts = []
    for g in range(n_groups):
        base = g * QKV_WIDTH
        parts.append(_pair_cols(w[:, base:base + Q_WIDTH]))
        parts.append(w[:, base + Q_WIDTH:base + QKV_WIDTH])
    return jnp.concatenate(parts, axis=1).astype(BF16)


def _out_weight(w):
    return jnp.swapaxes(w.reshape(2, 2, 4, HEAD_DIM, D_MODEL), 1, 2).reshape(Q_WIDTH, D_MODEL).astype(BF16)


def _block_diag_mean():
    idx = np.arange(MXU_TILE) // HEAD_DIM
    return jnp.asarray((idx[:, None] == idx[None, :]).astype(np.float32) / HEAD_DIM, BF16)


def _expand_matrix():
    col = np.arange(Q_WIDTH)
    p, r, e = col // PAIR_WIDTH, (col % PAIR_WIDTH) // LANES, (col % LANES) // HEAD_DIM
    src = e * HEAD_DIM + p * 4 + r
    m = (np.arange(LANES)[:, None] == src[None, :]).astype(np.float32)
    return jnp.asarray(np.concatenate([m, m], axis=0), BF16)


def _trunk(x, p):
    bn, s, _ = x.shape
    m = bn * s
    x2d = x.reshape(m, D_MODEL)

    q, k, v = _qkv_call(x2d, p["norm_attn"][0], p["a_w"], p["bd"], p["a_qg"], p["a_kg"], (1,))
    (o,) = _attn_call(q.reshape(bn, s, Q_WIDTH), k.reshape(bn, s, KV_WIDTH), v.reshape(bn, s, KV_WIDTH),
                      p["a_bias"], p["a_sink"], 1, A_HALF_WINDOW, False)
    x2d = _post_call([o.reshape(m, Q_WIDTH)], (1,), None, None, x2d, p["a_wo"], p["norm_ffn"][0],
                     p["wgu"][0], p["wd"][0])

    dils = tuple(d for _, d in DILATED_GROUPS)
    outs = _qkv_call(x2d, p["norm_attn"][1], p["b_w"], p["bd"], p["b_qg"], p["b_kg"], dils)
    os_, lses = [], []
    for gi, (window, dil) in enumerate(DILATED_GROUPS):
        q, k, v = outs[3 * gi:3 * gi + 3]
        ls = s // dil
        o, lse = _attn_call(q.reshape(bn, ls, dil * Q_WIDTH), k.reshape(bn, ls, dil * KV_WIDTH),
                            v.reshape(bn, ls, dil * KV_WIDTH), p["b_bias"][gi], None, dil,
                            window // (2 * dil), True)
        os_.append(o.reshape(m // dil, dil * Q_WIDTH))
        lses.append(lse.reshape(m // dil, dil * LANES))
    x2d = _post_call(os_, dils, lses, p["expand"], x2d, p["b_wo"], p["norm_ffn"][1], p["wgu"][1], p["wd"][1])
    return x2d.reshape(bn, s, D_MODEL)


def kernel(x_prompt, x_sample, rel_table, norm_attn, norm_ffn, a_w_qkv, a_q_gain, a_k_gain, a_sink, a_w_o,
           b_w_qkv, b_q_gain, b_k_gain, b_w_o, ffn_w_gate_up, ffn_w_down):
    n_grp = len(DILATED_GROUPS)
    scale = HEAD_DIM ** -0.5
    half = ATTN_HALF
    sink_rows = jnp.broadcast_to(a_sink[0].astype(F32).reshape(2, 2, 1, 4, 1),
                                 (2, 2, 1, 4, half)).reshape(2, 2, 1, 4 * half)
    p = {
        "norm_attn": norm_attn.astype(F32).reshape(-1, 1, D_MODEL),
        "norm_ffn": norm_ffn.astype(F32).reshape(-1, 1, D_MODEL),
        "bd": _block_diag_mean(),
        "expand": _expand_matrix(),
        "a_w": _qkv_weight(a_w_qkv[0], 1),
        "a_qg": (jnp.tile(a_q_gain[0].astype(F32), N_HEADS) * scale).reshape(1, 1, Q_WIDTH),
        "a_kg": jnp.tile(a_k_gain[0].astype(F32), N_KV_HEADS).reshape(1, 1, KV_WIDTH),
        "a_bias": _bias_table(rel_table, A_HALF_WINDOW, 1),
        "a_sink": sink_rows,
        "a_wo": _out_weight(a_w_o[0]),
        "b_w": _qkv_weight(b_w_qkv[0], n_grp),
        "b_qg": (jnp.tile(b_q_gain[0].astype(F32), (1, N_HEADS)) * scale).reshape(n_grp, 1, Q_WIDTH),
        "b_kg": jnp.tile(b_k_gain[0].astype(F32), (1, N_KV_HEADS)).reshape(n_grp, 1, KV_WIDTH),
        "b_bias": [_bias_table(rel_table, w // (2 * d), d) for w, d in DILATED_GROUPS],
        "b_wo": _out_weight(b_w_o[0]),
        "wgu": ffn_w_gate_up.astype(BF16),
        "wd": ffn_w_down.astype(BF16),
    }
    return _trunk(x_prompt, p), _trunk(x_sample, p)
```

```python
import functools
import math

import numpy as np
import jax
import jax.numpy as jnp
from jax import lax
from jax.experimental import pallas as pl
from jax.experimental.pallas import tpu as pltpu

D_MODEL = 1024
N_HEADS = 16
N_KV_HEADS = 4
HEAD_DIM = 64
Q_WIDTH = N_HEADS * HEAD_DIM
KV_WIDTH = N_KV_HEADS * HEAD_DIM
QKV_WIDTH = Q_WIDTH + 2 * KV_WIDTH
D_FF = 2816
A_HALF_WINDOW = 128
DILATED_GROUPS = ((128, 1), (512, 4), (2048, 16))
NUM_BUCKETS = 32
MAX_DISTANCE = 1024
EPS = 1e-6
NEG = -1e30

PAIR_WIDTH = Q_WIDTH // 2
LANES = 128
MXU_TILE = 256
ATTN_BLK = 128
ATTN_HALF = 64
ATTN_STEP_TOKENS = 2048
ATTN_OFFSETS = {64: (2, 4), 128: (2, 3)}
DEINTERLEAVE_STRIDE = 4
QKV_ROWS = 1024
POST_ROWS = 512
V7X_VMEM_BYTES = 64 * 1024 * 1024
VMEM_LIMIT = V7X_VMEM_BYTES * 7 // 8

BF16 = jnp.bfloat16
F32 = jnp.float32


def _dot(a, b):
    return jnp.dot(a, b, preferred_element_type=F32)


def _dot_nt(a, b):
    return lax.dot_general(a, b, (((1,), (1,)), ((), ())), preferred_element_type=F32)


def _dot_tn(a, b):
    return lax.dot_general(a, b, (((0,), (0,)), ((), ())), preferred_element_type=F32)


def _resident(shape):
    nd = len(shape)
    return pl.BlockSpec(shape, lambda *_: (0,) * nd, pipeline_mode=pl.Buffered(1))


def _qkv_kernel(x_ref, g_ref, w_ref, bd_ref, qg_ref, kg_ref, *refs, dils):
    out_refs, stages, stages2 = refs[:-2], refs[-2], refs[-1]
    tm = x_ref.shape[0]
    x = x_ref[...]
    ms = jnp.mean(x * x, axis=-1, keepdims=True)
    h = (x * lax.rsqrt(ms + EPS) * g_ref[...]).astype(BF16)
    bd = bd_ref[...]

    def emit(ref, dil, width, lo, val, slot):
        if dil == 1:
            ref[:, lo:lo + MXU_TILE] = val.astype(BF16)
            return
        stage, stage2 = stages.at[slot], stages2.at[slot]
        planes = range(MXU_TILE // LANES)
        for c in planes:
            stage[c] = val[:, c * LANES:(c + 1) * LANES]
        if dil <= DEINTERLEAVE_STRIDE:
            for r in range(dil):
                for c in planes:
                    col = r * width + lo + c * LANES
                    ref[:, col:col + LANES] = stage[c, pl.ds(r, tm // dil, stride=dil), :].astype(BF16)
            return
        s0, s1 = DEINTERLEAVE_STRIDE, dil // DEINTERLEAVE_STRIDE
        for r0 in range(s0):
            for c in planes:
                stage2[c, r0] = stage[c, pl.ds(r0, tm // s0, stride=s0), :]
        for r0 in range(s0):
            for r1 in range(s1):
                for c in planes:
                    col = (s0 * r1 + r0) * width + lo + c * LANES
                    ref[:, col:col + LANES] = stage2[c, r0, pl.ds(r1, tm // dil, stride=s1), :].astype(BF16)

    n_cols = QKV_WIDTH // MXU_TILE
    n_norm = (Q_WIDTH + KV_WIDTH) // MXU_TILE
    units = [(g, c) for g in reversed(range(len(dils))) for c in range(n_cols)]
    proj, sumsq = {}, {}

    def project(u):
        g, c = u
        lo = g * QKV_WIDTH + c * MXU_TILE
        proj[u] = _dot(h, w_ref[:, lo:lo + MXU_TILE])

    def square_sum(u):
        if u[1] < n_norm:
            t = proj[u]
            sumsq[u] = _dot((t * t).astype(BF16), bd)

    def write(i):
        g, c = u = units[i]
        q_ref, k_ref, v_ref = out_refs[3 * g:3 * g + 3]
        val = proj.pop(u)
        lo = c * MXU_TILE
        if c < n_norm:
            gain = qg_ref[g, :, lo:lo + MXU_TILE] if lo < Q_WIDTH else kg_ref[g]
            val = val * lax.rsqrt(sumsq.pop(u) + EPS) * gain
        ref, width, lo = ((q_ref, Q_WIDTH, lo) if lo < Q_WIDTH else
                          (k_ref, KV_WIDTH, 0) if c < n_norm else (v_ref, KV_WIDTH, 0))
        emit(ref, dils[g], width, lo, val, i % 2)

    n = len(units)
    for t in range(n + 2):
        if t < n:
            project(units[t])
        if 0 <= t - 1 < n:
            square_sum(units[t - 1])
        if 0 <= t - 2 < n:
            write(t - 2)


def _qkv_call(x2d, gain, w, bd, qg, kg, dils, tm=QKV_ROWS):
    m = x2d.shape[0]
    out_shape, out_specs = [], []
    for dil in dils:
        for width in (Q_WIDTH, KV_WIDTH, KV_WIDTH):
            out_shape.append(jax.ShapeDtypeStruct((m // dil, dil * width), BF16))
            out_specs.append(pl.BlockSpec((tm // dil, dil * width), lambda i: (i, 0)))
    return pl.pallas_call(
        functools.partial(_qkv_kernel, dils=tuple(dils)),
        grid=(m // tm,),
        in_specs=[pl.BlockSpec((tm, D_MODEL), lambda i: (i, 0)), _resident(gain.shape), _resident(w.shape),
                  _resident(bd.shape), _resident(qg.shape), _resident(kg.shape)],
        out_specs=out_specs,
        out_shape=out_shape,
        scratch_shapes=[pltpu.VMEM((2, MXU_TILE // LANES, tm, LANES), F32),
                        pltpu.VMEM((2, MXU_TILE // LANES, DEINTERLEAVE_STRIDE, tm // DEINTERLEAVE_STRIDE, LANES),
                                   F32)],
        compiler_params=pltpu.CompilerParams(dimension_semantics=("parallel",),
                                             vmem_limit_bytes=VMEM_LIMIT),
        name=f"qkv{len(dils)}",
    )(x2d, gain, w, bd, qg, kg)


def _attn_kernel(*refs, tq, hw, n_res, n_blocks, has_sink, has_lse):
    q_ref, kl_ref, kc_ref, kr_ref, vl_ref, vc_ref, vr_ref, bias_ref = refs[:8]
    refs = refs[8:]
    sink_ref = None
    if has_sink:
        sink_ref, refs = refs[0], refs[1:]
    o_ref, refs = refs[0], refs[1:]
    lse_ref = None
    if has_lse:
        lse_ref, refs = refs[0], refs[1:]
    kw, vw = refs

    blk = ATTN_BLK
    half = ATTN_HALF
    win = half + 2 * hw

    for dst, (l, c, r) in ((kw, (kl_ref, kc_ref, kr_ref)), (vw, (vl_ref, vc_ref, vr_ref))):
        dst[0:hw, :] = l[...]
        dst[hw:hw + tq, :] = c[...]
        dst[hw + tq:hw + tq + hw, :] = r[...]

    tile = pl.program_id(2)
    n_inner = tq // blk
    lane = lax.broadcasted_iota(jnp.int32, (1, LANES), 1)
    keep = (lane < HEAD_DIM, lane >= HEAD_DIM)
    first_half = lax.broadcasted_iota(jnp.int32, (LANES, 1), 0) < HEAD_DIM
    sub = lax.broadcasted_iota(jnp.int32, (8, 1), 0)

    chunks = [(j, u, p, h, e) for u in range(n_res) for j in range(n_inner) for p in range(2)
              for h in range(2) for e in range(2)]
    kinds, lse_rows = {}, {}
    for j in range(n_inner):
        gb = tile * n_inner + j
        kinds[j] = 2 * (gb == 0).astype(jnp.int32) + (gb == n_blocks - 1).astype(jnp.int32)
    scores_, probs_, results_ = {}, {}, {}

    def scores(c):
        j, u, p, h, e = c
        r0 = j * blk + h * half
        kp = kw[r0:r0 + win, u * KV_WIDTH + p * LANES:u * KV_WIDTH + (p + 1) * LANES]
        base = u * Q_WIDTH + p * PAIR_WIDTH
        qcat = jnp.concatenate([q_ref[r0:r0 + half, base + r * LANES:base + (r + 1) * LANES]
                                for r in range(4)], axis=0)
        k_e = jnp.where(keep[e], kp, jnp.zeros((), BF16))
        scores_[c] = _dot_nt(k_e, qcat) + bias_ref[kinds[j], p, e, h]

    def probs(c):
        j, u, p, h, e = c
        st = scores_.pop(c)
        m = jnp.max(st, axis=0, keepdims=True)
        sk = None
        if has_sink:
            sk = sink_ref[p, e]
            m = jnp.maximum(m, sk)
        probs_[c] = (jnp.exp((st - m).astype(BF16)), m, sk)

    def values(c):
        j, u, p, h, e = c
        pt, m, sk = probs_.pop(c)
        r0 = j * blk + h * half
        vp = vw[r0:r0 + win, u * KV_WIDTH + p * LANES:u * KV_WIDTH + (p + 1) * LANES]
        v_e = jnp.where(keep[e], vp, jnp.ones((), BF16))
        ot = _dot_tn(v_e, pt)
        den = ot[HEAD_DIM:HEAD_DIM + 1] if e == 0 else ot[0:1]
        if has_sink:
            den = den + jnp.exp(sk - m)
        results_[c] = (ot, 1.0 / den, (m + jnp.log(den)) if has_lse else None)
        if e == 1:
            finish(j, u, p, h)

    def finish(j, u, p, h):
        (o0, i0, l0), (o1, i1, l1) = results_.pop((j, u, p, h, 0)), results_.pop((j, u, p, h, 1))
        zeros = jnp.zeros((8, blk), F32)
        lse_lo, lse_hi = lse_rows.get((j, u), (zeros, zeros))
        r0 = j * blk + h * half
        for rp in range(2):
            cs = slice(rp * LANES, (rp + 1) * LANES)
            t = jnp.where(first_half, o0[:, cs] * i0[:, cs], o1[:, cs] * i1[:, cs])
            tt = t.T.astype(BF16)
            for s in range(2):
                col = u * Q_WIDTH + p * PAIR_WIDTH + (2 * rp + s) * LANES
                o_ref[r0:r0 + half, col:col + LANES] = tt[s * half:(s + 1) * half]
            if has_lse:
                v0, v1 = l0[:, cs], l1[:, cs]
                w0, w1 = pltpu.roll(v0, HEAD_DIM, axis=1), pltpu.roll(v1, HEAD_DIM, axis=1)
                for s in range(2):
                    mask = (sub == p * 4 + 2 * rp + s) & keep[h]
                    lse_lo = jnp.where(mask, v0 if s == h else w0, lse_lo)
                    lse_hi = jnp.where(mask, v1 if s == h else w1, lse_hi)
        lse_rows[(j, u)] = (lse_lo, lse_hi)
        if has_lse and p == 1 and h == 1:
            z = jnp.zeros((HEAD_DIM - 8, blk), F32)
            lse_ref[j * blk:(j + 1) * blk, u * LANES:(u + 1) * LANES] = (
                jnp.concatenate([lse_lo, z, lse_hi, z], axis=0).T)

    n = len(chunks)
    d1, d2 = ATTN_OFFSETS[hw]
    for t in range(n + d2):
        if t < n:
            scores(chunks[t])
        if 0 <= t - d1 < n:
            probs(chunks[t - d1])
        if 0 <= t - d2 < n:
            values(chunks[t - d2])


def _attn_call(q, k, v, bias, sink_rows, dil, hw, has_lse):
    bn, ls, _ = q.shape
    blk = ATTN_BLK
    tq = min(ATTN_STEP_TOKENS, ls)
    n_res = ATTN_STEP_TOKENS // tq
    assert ls % tq == 0 and tq % blk == 0 and tq % hw == 0 and blk % hw == 0 and dil % n_res == 0
    n_tiles = ls // tq
    ratio = tq // hw
    last_halo = ls // hw - 1

    center = lambda b, r, i: (b, i, r)
    left = lambda b, r, i: (b, jnp.maximum(i * ratio - 1, 0), r)
    right = lambda b, r, i: (b, jnp.minimum((i + 1) * ratio, last_halo), r)
    kv_specs = [pl.BlockSpec((None, hw, n_res * KV_WIDTH), left),
                pl.BlockSpec((None, tq, n_res * KV_WIDTH), center),
                pl.BlockSpec((None, hw, n_res * KV_WIDTH), right)]
    in_specs = ([pl.BlockSpec((None, tq, n_res * Q_WIDTH), center)] + kv_specs + kv_specs
                + [_resident(bias.shape)])
    args = [q, k, k, k, v, v, v, bias]
    if sink_rows is not None:
        in_specs.append(_resident(sink_rows.shape))
        args.append(sink_rows)
    out_shape = [jax.ShapeDtypeStruct(q.shape, BF16)]
    out_specs = [pl.BlockSpec((None, tq, n_res * Q_WIDTH), center)]
    if has_lse:
        out_shape.append(jax.ShapeDtypeStruct((bn, ls, dil * LANES), F32))
        out_specs.append(pl.BlockSpec((None, tq, n_res * LANES), center))
    win_rows = tq + 2 * hw
    return pl.pallas_call(
        functools.partial(_attn_kernel, tq=tq, hw=hw, n_res=n_res, n_blocks=ls // blk,
                          has_sink=sink_rows is not None, has_lse=has_lse),
        grid=(bn, dil // n_res, n_tiles),
        in_specs=in_specs,
        out_specs=out_specs,
        out_shape=out_shape,
        scratch_shapes=[pltpu.VMEM((win_rows, n_res * KV_WIDTH), BF16),
                        pltpu.VMEM((win_rows, n_res * KV_WIDTH), BF16)],
        compiler_params=pltpu.CompilerParams(dimension_semantics=("parallel", "parallel", "parallel"),
                                             vmem_limit_bytes=VMEM_LIMIT),
        name=f"attn_d{dil}",
    )(*args)


FFN_CHUNKS = ((0, 1536), (1536, 1280))


def _post_kernel(*refs, dils):
    n_groups = len(dils)
    o_refs, refs = refs[:n_groups], refs[n_groups:]
    if n_groups > 1:
        lse_refs, refs = refs[:n_groups], refs[n_groups:]
        e_ref, refs = refs[0], refs[1:]
    x_ref, wo_ref, g_ref, wgu_ref, wd_ref, out_ref = refs[:6]
    scratch = list(refs[6:])
    stages, stages2 = {}, {}
    for g, dil in enumerate(dils):
        if dil > 1:
            stages[g] = scratch.pop(0)
        if dil > DEINTERLEAVE_STRIDE:
            stages2[g] = scratch.pop(0)
    tm = x_ref.shape[0]

    def token_rows(ref, g, width, plane0):
        dil = dils[g]
        if dil == 1:
            return ref[...].astype(F32)
        stage = stages[g]
        planes = [plane0 + c for c in range(width // LANES)]
        piece = lambda r, c: ref[:, r * width + c * LANES:r * width + (c + 1) * LANES].astype(F32)
        if dil <= DEINTERLEAVE_STRIDE:
            for r in range(dil):
                for c, pc in enumerate(planes):
                    stage[pc, pl.ds(r, tm // dil, stride=dil), :] = piece(r, c)
        else:
            stage2 = stages2[g]
            s0, s1 = DEINTERLEAVE_STRIDE, dil // DEINTERLEAVE_STRIDE
            for r0 in range(s0):
                for r1 in range(s1):
                    for c, pc in enumerate(planes):
                        stage2[pc, r0, pl.ds(r1, tm // dil, stride=s1), :] = piece(s0 * r1 + r0, c)
            for r0 in range(s0):
                for pc in planes:
                    stage[pc, pl.ds(r0, tm // s0, stride=s0), :] = stage2[pc, r0]
        return jnp.concatenate([stage[pc] for pc in planes], axis=1)

    if n_groups == 1:
        o = o_refs[0][...]
    else:
        lses = [token_rows(lse_refs[g], g, LANES, Q_WIDTH // LANES) for g in range(n_groups)]
        mx = functools.reduce(jnp.maximum, lses)
        ex = [jnp.exp(l - mx) for l in lses]
        inv = 1.0 / functools.reduce(jnp.add, ex)
        last = token_rows(o_refs[n_groups - 1], n_groups - 1, Q_WIDTH, 0)
        o = last
        for g in range(n_groups - 1):
            wt = ex[g] * inv
            hi = wt.astype(BF16)
            lo = (wt - hi.astype(F32)).astype(BF16)
            wexp = _dot(jnp.concatenate([hi, lo], axis=-1), e_ref[...])
            o = o + wexp * (token_rows(o_refs[g], g, Q_WIDTH, 0) - last)
        o = o.astype(BF16)

    x1 = x_ref[...] + _dot(o, wo_ref[...])
    ms = jnp.mean(x1 * x1, axis=-1, keepdims=True)
    h = (x1 * lax.rsqrt(ms + EPS) * g_ref[...]).astype(BF16)
    acc = x1
    for c0, cw in FFN_CHUNKS:
        gate = _dot(h, wgu_ref[:, c0:c0 + cw])
        up = _dot(h, wgu_ref[:, D_FF + c0:D_FF + c0 + cw])
        act = (gate * jax.nn.sigmoid(gate) * up).astype(BF16)
        acc = acc + _dot(act, wd_ref[c0:c0 + cw, :])
    out_ref[...] = acc


def _post_call(os_, dils, lses, expand, x2d, wo, gain, wgu, wd, tm=POST_ROWS):
    m = x2d.shape[0]
    n_groups = len(os_)
    row = lambda width: pl.BlockSpec((tm, width), lambda i: (i, 0))
    in_specs = [pl.BlockSpec((tm // dil, dil * Q_WIDTH), lambda i: (i, 0)) for dil in dils]
    args = list(os_)
    if n_groups > 1:
        in_specs += [pl.BlockSpec((tm // dil, dil * LANES), lambda i: (i, 0)) for dil in dils]
        in_specs += [_resident(expand.shape)]
        args += list(lses) + [expand]
    in_specs += [row(D_MODEL), _resident(wo.shape), _resident(gain.shape), _resident(wgu.shape),
                 _resident(wd.shape)]
    args += [x2d, wo, gain, wgu, wd]
    n_planes = Q_WIDTH // LANES + 1
    scratch = []
    for dil in dils:
        if dil > 1:
            scratch.append(pltpu.VMEM((n_planes, tm, LANES), F32))
        if dil > DEINTERLEAVE_STRIDE:
            scratch.append(pltpu.VMEM((n_planes, DEINTERLEAVE_STRIDE, tm // DEINTERLEAVE_STRIDE, LANES), F32))
    return pl.pallas_call(
        functools.partial(_post_kernel, dils=tuple(dils)),
        grid=(m // tm,),
        in_specs=in_specs,
        out_specs=row(D_MODEL),
        out_shape=jax.ShapeDtypeStruct((m, D_MODEL), F32),
        scratch_shapes=scratch,
        compiler_params=pltpu.CompilerParams(dimension_semantics=("parallel",),
                                             vmem_limit_bytes=VMEM_LIMIT),
        name=f"post{n_groups}",
    )(*args)


def _t5_buckets(rel):
    nb = NUM_BUCKETS // 2
    max_exact = nb // 2
    n = np.abs(rel)
    large = max_exact + (np.log(np.maximum(n, 1) / max_exact)
                         / math.log(MAX_DISTANCE / max_exact) * (nb - max_exact)).astype(np.int32)
    large = np.minimum(large, nb - 1)
    return ((rel > 0).astype(np.int32) * nb + np.where(n < max_exact, n, large)).astype(np.int32)


def _bias_table(rel_table, hw, dil):
    blk, half = ATTN_BLK, ATTN_HALF
    win = half + 2 * hw
    rel = np.arange(win)[None, :] - hw - np.arange(half)[:, None]
    band = np.abs(rel) <= hw
    kj = np.arange(win)[None, :]
    masks = []
    for first in (False, True):
        for last in (False, True):
            per_half = []
            for h in range(2):
                ok = band.copy()
                if first:
                    ok &= kj >= hw - h * half
                if last:
                    ok &= kj < hw + blk - h * half
                per_half.append(ok)
            masks.append(np.stack(per_half))
    mask = jnp.asarray(np.stack(masks))[:, None, None, :, None]
    n = win + half - 1
    diff = np.arange(n)
    diff = np.where(diff < win, diff, diff - n) - hw
    w = jnp.take(rel_table.astype(F32), jnp.asarray(_t5_buckets(dil * diff)), axis=0).T
    tb = jnp.tile(w, (1, half))[:, :half * (n - 1)].reshape(N_HEADS, half, n - 1)[:, :, :win]
    tb = tb.reshape(1, 2, 2, 1, 4, half, win)
    bias = jnp.where(mask, tb, NEG)
    return bias.reshape(4, 2, 2, 2, 4 * half, win).swapaxes(-1, -2)


def _pair_cols(w):
    lead = w.shape[:-1]
    return jnp.swapaxes(w.reshape(*lead, 2, 2, 4, HEAD_DIM), -3, -2).reshape(*lead, Q_WIDTH)


def _qkv_weight(w, n_groups):
    parts = []
    for g in range(n_groups):
        base = g * QKV_WIDTH
        parts.append(_pair_cols(w[:, base:base + Q_WIDTH]))
        parts.append(w[:, base + Q_WIDTH:base + QKV_WIDTH])
    return jnp.concatenate(parts, axis=1).astype(BF16)


def _out_weight(w):
    return jnp.swapaxes(w.reshape(2, 2, 4, HEAD_DIM, D_MODEL), 1, 2).reshape(Q_WIDTH, D_MODEL).astype(BF16)


def _block_diag_mean():
    idx = np.arange(MXU_TILE) // HEAD_DIM
    return jnp.asarray((idx[:, None] == idx[None, :]).astype(np.float32) / HEAD_DIM, BF16)


def _expand_matrix():
    col = np.arange(Q_WIDTH)
    p, r, e = col // PAIR_WIDTH, (col % PAIR_WIDTH) // LANES, (col % LANES) // HEAD_DIM
    src = e * HEAD_DIM + p * 4 + r
    m = (np.arange(LANES)[:, None] == src[None, :]).astype(np.float32)
    return jnp.asarray(np.concatenate([m, m], axis=0), BF16)


def _trunk(x, p):
    bn, s, _ = x.shape
    m = bn * s
    x2d = x.reshape(m, D_MODEL)

    q, k, v = _qkv_call(x2d, p["norm_attn"][0], p["a_w"], p["bd"], p["a_qg"], p["a_kg"], (1,))
    (o,) = _attn_call(q.reshape(bn, s, Q_WIDTH), k.reshape(bn, s, KV_WIDTH), v.reshape(bn, s, KV_WIDTH),
                      p["a_bias"], p["a_sink"], 1, A_HALF_WINDOW, False)
    x2d = _post_call([o.reshape(m, Q_WIDTH)], (1,), None, None, x2d, p["a_wo"], p["norm_ffn"][0],
                     p["wgu"][0], p["wd"][0])

    dils = tuple(d for _, d in DILATED_GROUPS)
    outs = _qkv_call(x2d, p["norm_attn"][1], p["b_w"], p["bd"], p["b_qg"], p["b_kg"], dils)
    os_, lses = [], []
    for gi, (window, dil) in enumerate(DILATED_GROUPS):
        q, k, v = outs[3 * gi:3 * gi + 3]
        ls = s // dil
        o, lse = _attn_call(q.reshape(bn, ls, dil * Q_WIDTH), k.reshape(bn, ls, dil * KV_WIDTH),
                            v.reshape(bn, ls, dil * KV_WIDTH), p["b_bias"][gi], None, dil,
                            window // (2 * dil), True)
        os_.append(o.reshape(m // dil, dil * Q_WIDTH))
        lses.append(lse.reshape(m // dil, dil * LANES))
    x2d = _post_call(os_, dils, lses, p["expand"], x2d, p["b_wo"], p["norm_ffn"][1], p["wgu"][1], p["wd"][1])
    return x2d.reshape(bn, s, D_MODEL)


def kernel(x_prompt, x_sample, rel_table, norm_attn, norm_ffn, a_w_qkv, a_q_gain, a_k_gain, a_sink, a_w_o,
           b_w_qkv, b_q_gain, b_k_gain, b_w_o, ffn_w_gate_up, ffn_w_down):
    n_grp = len(DILATED_GROUPS)
    assert norm_attn.shape == (2, D_MODEL) and norm_ffn.shape == (2, D_MODEL)
    assert a_w_qkv.shape == (1, D_MODEL, QKV_WIDTH) and b_w_qkv.shape == (1, D_MODEL, n_grp * QKV_WIDTH)
    assert ffn_w_gate_up.shape == (2, D_MODEL, 2 * D_FF) and ffn_w_down.shape == (2, D_FF, D_MODEL)
    for x in (x_prompt, x_sample):
        assert x.shape[-1] == D_MODEL and x.shape[1] % (ATTN_BLK * DILATED_GROUPS[-1][1]) == 0
        assert (x.shape[0] * x.shape[1]) % max(QKV_ROWS, POST_ROWS) == 0
    scale = HEAD_DIM ** -0.5
    half = ATTN_HALF
    sink_rows = jnp.broadcast_to(a_sink[0].astype(F32).reshape(2, 2, 1, 4, 1),
                                 (2, 2, 1, 4, half)).reshape(2, 2, 1, 4 * half)
    p = {
        "norm_attn": norm_attn.astype(F32).reshape(-1, 1, D_MODEL),
        "norm_ffn": norm_ffn.astype(F32).reshape(-1, 1, D_MODEL),
        "bd": _block_diag_mean(),
        "expand": _expand_matrix(),
        "a_w": _qkv_weight(a_w_qkv[0], 1),
        "a_qg": (jnp.tile(a_q_gain[0].astype(F32), N_HEADS) * scale).reshape(1, 1, Q_WIDTH),
        "a_kg": jnp.tile(a_k_gain[0].astype(F32), N_KV_HEADS).reshape(1, 1, KV_WIDTH),
        "a_bias": _bias_table(rel_table, A_HALF_WINDOW, 1),
        "a_sink": sink_rows,
        "a_wo": _out_weight(a_w_o[0]),
        "b_w": _qkv_weight(b_w_qkv[0], n_grp),
        "b_qg": (jnp.tile(b_q_gain[0].astype(F32), (1, N_HEADS)) * scale).reshape(n_grp, 1, Q_WIDTH),
        "b_kg": jnp.tile(b_k_gain[0].astype(F32), (1, N_KV_HEADS)).reshape(n_grp, 1, KV_WIDTH),
        "b_bias": [_bias_table(rel_table, w // (2 * d), d) for w, d in DILATED_GROUPS],
        "b_wo": _out_weight(b_w_o[0]),
        "wgu": ffn_w_gate_up.astype(BF16),
        "wd": ffn_w_down.astype(BF16),
    }
    return _trunk(x_prompt, p), _trunk(x_sample, p)
```

```python
import functools
import math

import numpy as np
import jax
import jax.numpy as jnp
from jax import lax
from jax.experimental import pallas as pl
from jax.experimental.pallas import tpu as pltpu

D_MODEL = 1024
N_HEADS = 16
N_KV_HEADS = 4
HEAD_DIM = 64
Q_WIDTH = N_HEADS * HEAD_DIM
KV_WIDTH = N_KV_HEADS * HEAD_DIM
QKV_WIDTH = Q_WIDTH + 2 * KV_WIDTH
D_FF = 2816
A_HALF_WINDOW = 128
DILATED_GROUPS = ((128, 1), (512, 4), (2048, 16))
NUM_BUCKETS = 32
MAX_DISTANCE = 1024
EPS = 1e-6
NEG = -1e30

PAIR_WIDTH = Q_WIDTH // 2
LANES = 128
MXU_TILE = 256
ATTN_BLK = 128
ATTN_HALF = 64
ATTN_STEP_TOKENS = 2048
ATTN_OFFSETS = {64: (3, 6), 128: (2, 4)}
DEINTERLEAVE_STRIDE = 4
QKV_ROWS = 1024
POST_ROWS = 512
V7X_VMEM_BYTES = 64 * 1024 * 1024
VMEM_LIMIT = V7X_VMEM_BYTES * 7 // 8

BF16 = jnp.bfloat16
F32 = jnp.float32


def _dot(a, b):
    return jnp.dot(a, b, preferred_element_type=F32)


def _dot_nt(a, b):
    return lax.dot_general(a, b, (((1,), (1,)), ((), ())), preferred_element_type=F32)


def _dot_tn(a, b):
    return lax.dot_general(a, b, (((0,), (0,)), ((), ())), preferred_element_type=F32)


def _resident(shape):
    nd = len(shape)
    return pl.BlockSpec(shape, lambda *_: (0,) * nd, pipeline_mode=pl.Buffered(1))


def _qkv_kernel(x_ref, g_ref, w_ref, bd_ref, qg_ref, kg_ref, *refs, dils):
    out_refs, stages, stages2 = refs[:-2], refs[-2], refs[-1]
    tm = x_ref.shape[0]
    x = x_ref[...]
    ms = jnp.mean(x * x, axis=-1, keepdims=True)
    h = (x * lax.rsqrt(ms + EPS) * g_ref[...]).astype(BF16)
    bd = bd_ref[...]

    def emit(ref, dil, width, lo, val, slot):
        if dil == 1:
            ref[:, lo:lo + MXU_TILE] = val.astype(BF16)
            return
        stage, stage2 = stages.at[slot], stages2.at[slot]
        planes = range(MXU_TILE // LANES)
        for c in planes:
            stage[c] = val[:, c * LANES:(c + 1) * LANES]
        if dil <= DEINTERLEAVE_STRIDE:
            for r in range(dil):
                for c in planes:
                    col = r * width + lo + c * LANES
                    ref[:, col:col + LANES] = stage[c, pl.ds(r, tm // dil, stride=dil), :].astype(BF16)
            return
        s0, s1 = DEINTERLEAVE_STRIDE, dil // DEINTERLEAVE_STRIDE
        for r0 in range(s0):
            for c in planes:
                stage2[c, r0] = stage[c, pl.ds(r0, tm // s0, stride=s0), :]
        for r0 in range(s0):
            for r1 in range(s1):
                for c in planes:
                    col = (s0 * r1 + r0) * width + lo + c * LANES
                    ref[:, col:col + LANES] = stage2[c, r0, pl.ds(r1, tm // dil, stride=s1), :].astype(BF16)

    n_cols = QKV_WIDTH // MXU_TILE
    n_norm = (Q_WIDTH + KV_WIDTH) // MXU_TILE
    units = [(g, c) for g in reversed(range(len(dils))) for c in range(n_cols)]
    proj, sumsq = {}, {}

    def project(u):
        g, c = u
        lo = g * QKV_WIDTH + c * MXU_TILE
        proj[u] = _dot(h, w_ref[:, lo:lo + MXU_TILE])

    def square_sum(u):
        if u[1] < n_norm:
            t = proj[u]
            sumsq[u] = _dot((t * t).astype(BF16), bd)

    def write(i):
        g, c = u = units[i]
        q_ref, k_ref, v_ref = out_refs[3 * g:3 * g + 3]
        val = proj.pop(u)
        lo = c * MXU_TILE
        if c < n_norm:
            gain = qg_ref[g, :, lo:lo + MXU_TILE] if lo < Q_WIDTH else kg_ref[g]
            val = val * lax.rsqrt(sumsq.pop(u) + EPS) * gain
        ref, width, lo = ((q_ref, Q_WIDTH, lo) if lo < Q_WIDTH else
                          (k_ref, KV_WIDTH, 0) if c < n_norm else (v_ref, KV_WIDTH, 0))
        emit(ref, dils[g], width, lo, val, i % 2)

    n = len(units)
    for t in range(n + 2):
        if t < n:
            project(units[t])
        if 0 <= t - 1 < n:
            square_sum(units[t - 1])
        if 0 <= t - 2 < n:
            write(t - 2)


def _qkv_call(x2d, gain, w, bd, qg, kg, dils, tm=QKV_ROWS):
    m = x2d.shape[0]
    out_shape, out_specs = [], []
    for dil in dils:
        for width in (Q_WIDTH, KV_WIDTH, KV_WIDTH):
            out_shape.append(jax.ShapeDtypeStruct((m // dil, dil * width), BF16))
            out_specs.append(pl.BlockSpec((tm // dil, dil * width), lambda i: (i, 0)))
    return pl.pallas_call(
        functools.partial(_qkv_kernel, dils=tuple(dils)),
        grid=(m // tm,),
        in_specs=[pl.BlockSpec((tm, D_MODEL), lambda i: (i, 0)), _resident(gain.shape), _resident(w.shape),
                  _resident(bd.shape), _resident(qg.shape), _resident(kg.shape)],
        out_specs=out_specs,
        out_shape=out_shape,
        scratch_shapes=[pltpu.VMEM((2, MXU_TILE // LANES, tm, LANES), F32),
                        pltpu.VMEM((2, MXU_TILE // LANES, DEINTERLEAVE_STRIDE, tm // DEINTERLEAVE_STRIDE, LANES),
                                   F32)],
        compiler_params=pltpu.CompilerParams(dimension_semantics=("parallel",),
                                             vmem_limit_bytes=VMEM_LIMIT),
        name=f"qkv{len(dils)}",
    )(x2d, gain, w, bd, qg, kg)


def _attn_kernel(*refs, tq, hw, n_res, n_blocks, has_sink, has_lse):
    q_ref, kl_ref, kc_ref, kr_ref, vl_ref, vc_ref, vr_ref, bias_ref = refs[:8]
    refs = refs[8:]
    sink_ref = None
    if has_sink:
        sink_ref, refs = refs[0], refs[1:]
    o_ref, refs = refs[0], refs[1:]
    lse_ref = None
    if has_lse:
        lse_ref, refs = refs[0], refs[1:]
    kw, vw = refs

    blk = ATTN_BLK
    half = ATTN_HALF
    win = half + 2 * hw

    for dst, (l, c, r) in ((kw, (kl_ref, kc_ref, kr_ref)), (vw, (vl_ref, vc_ref, vr_ref))):
        dst[0:hw, :] = l[...]
        dst[hw:hw + tq, :] = c[...]
        dst[hw + tq:hw + tq + hw, :] = r[...]

    tile = pl.program_id(2)
    n_inner = tq // blk
    lane = lax.broadcasted_iota(jnp.int32, (1, LANES), 1)
    keep = (lane < HEAD_DIM, lane >= HEAD_DIM)
    first_half = lax.broadcasted_iota(jnp.int32, (LANES, 1), 0) < HEAD_DIM
    sub = lax.broadcasted_iota(jnp.int32, (8, 1), 0)

    chunks = [(j, u, p, h, e) for u in range(n_res) for j in range(n_inner) for p in range(2)
              for h in range(2) for e in range(2)]
    kinds, lse_rows = {}, {}
    for j in range(n_inner):
        gb = tile * n_inner + j
        kinds[j] = 2 * (gb == 0).astype(jnp.int32) + (gb == n_blocks - 1).astype(jnp.int32)
    scores_, probs_, results_ = {}, {}, {}

    def scores(c):
        j, u, p, h, e = c
        r0 = j * blk + h * half
        kp = kw[r0:r0 + win, u * KV_WIDTH + p * LANES:u * KV_WIDTH + (p + 1) * LANES]
        base = u * Q_WIDTH + p * PAIR_WIDTH
        qcat = jnp.concatenate([q_ref[r0:r0 + half, base + r * LANES:base + (r + 1) * LANES]
                                for r in range(4)], axis=0)
        k_e = jnp.where(keep[e], kp, jnp.zeros((), BF16))
        scores_[c] = _dot_nt(k_e, qcat) + bias_ref[kinds[j], p, e, h]

    def probs(c):
        j, u, p, h, e = c
        st = scores_.pop(c)
        m = jnp.max(st, axis=0, keepdims=True)
        sk = None
        if has_sink:
            sk = sink_ref[p, e]
            m = jnp.maximum(m, sk)
        probs_[c] = (jnp.exp((st - m).astype(BF16)), m, sk)

    def values(c):
        j, u, p, h, e = c
        pt, m, sk = probs_.pop(c)
        r0 = j * blk + h * half
        vp = vw[r0:r0 + win, u * KV_WIDTH + p * LANES:u * KV_WIDTH + (p + 1) * LANES]
        v_e = jnp.where(keep[e], vp, jnp.ones((), BF16))
        ot = _dot_tn(v_e, pt)
        den = ot[HEAD_DIM:HEAD_DIM + 1] if e == 0 else ot[0:1]
        if has_sink:
            den = den + jnp.exp(sk - m)
        results_[c] = (ot, 1.0 / den, (m + jnp.log(den)) if has_lse else None)
        if e == 1:
            finish(j, u, p, h)

    def finish(j, u, p, h):
        (o0, i0, l0), (o1, i1, l1) = results_.pop((j, u, p, h, 0)), results_.pop((j, u, p, h, 1))
        zeros = jnp.zeros((8, blk), F32)
        lse_lo, lse_hi = lse_rows.get((j, u), (zeros, zeros))
        r0 = j * blk + h * half
        for rp in range(2):
            cs = slice(rp * LANES, (rp + 1) * LANES)
            t = jnp.where(first_half, o0[:, cs] * i0[:, cs], o1[:, cs] * i1[:, cs])
            tt = t.T.astype(BF16)
            for s in range(2):
                col = u * Q_WIDTH + p * PAIR_WIDTH + (2 * rp + s) * LANES
                o_ref[r0:r0 + half, col:col + LANES] = tt[s * half:(s + 1) * half]
            if has_lse:
                v0, v1 = l0[:, cs], l1[:, cs]
                w0, w1 = pltpu.roll(v0, HEAD_DIM, axis=1), pltpu.roll(v1, HEAD_DIM, axis=1)
                for s in range(2):
                    mask = (sub == p * 4 + 2 * rp + s) & keep[h]
                    lse_lo = jnp.where(mask, v0 if s == h else w0, lse_lo)
                    lse_hi = jnp.where(mask, v1 if s == h else w1, lse_hi)
        lse_rows[(j, u)] = (lse_lo, lse_hi)
        if has_lse and p == 1 and h == 1:
            z = jnp.zeros((HEAD_DIM - 8, blk), F32)
            lse_ref[j * blk:(j + 1) * blk, u * LANES:(u + 1) * LANES] = (
                jnp.concatenate([lse_lo, z, lse_hi, z], axis=0).T)

    n = len(chunks)
    d1, d2 = ATTN_OFFSETS[hw]
    for t in range(n + d2):
        if t < n:
            scores(chunks[t])
        if 0 <= t - d1 < n:
            probs(chunks[t - d1])
        if 0 <= t - d2 < n:
            values(chunks[t - d2])


def _attn_call(q, k, v, bias, sink_rows, dil, hw, has_lse):
    bn, ls, _ = q.shape
    blk = ATTN_BLK
    tq = min(ATTN_STEP_TOKENS, ls)
    n_res = ATTN_STEP_TOKENS // tq
    assert ls % tq == 0 and tq % blk == 0 and tq % hw == 0 and blk % hw == 0 and dil % n_res == 0
    n_tiles = ls // tq
    ratio = tq // hw
    last_halo = ls // hw - 1

    center = lambda b, r, i: (b, i, r)
    left = lambda b, r, i: (b, jnp.maximum(i * ratio - 1, 0), r)
    right = lambda b, r, i: (b, jnp.minimum((i + 1) * ratio, last_halo), r)
    kv_specs = [pl.BlockSpec((None, hw, n_res * KV_WIDTH), left),
                pl.BlockSpec((None, tq, n_res * KV_WIDTH), center),
                pl.BlockSpec((None, hw, n_res * KV_WIDTH), right)]
    in_specs = ([pl.BlockSpec((None, tq, n_res * Q_WIDTH), center)] + kv_specs + kv_specs
                + [_resident(bias.shape)])
    args = [q, k, k, k, v, v, v, bias]
    if sink_rows is not None:
        in_specs.append(_resident(sink_rows.shape))
        args.append(sink_rows)
    out_shape = [jax.ShapeDtypeStruct(q.shape, BF16)]
    out_specs = [pl.BlockSpec((None, tq, n_res * Q_WIDTH), center)]
    if has_lse:
        out_shape.append(jax.ShapeDtypeStruct((bn, ls, dil * LANES), F32))
        out_specs.append(pl.BlockSpec((None, tq, n_res * LANES), center))
    win_rows = tq + 2 * hw
    return pl.pallas_call(
        functools.partial(_attn_kernel, tq=tq, hw=hw, n_res=n_res, n_blocks=ls // blk,
                          has_sink=sink_rows is not None, has_lse=has_lse),
        grid=(bn, dil // n_res, n_tiles),
        in_specs=in_specs,
        out_specs=out_specs,
        out_shape=out_shape,
        scratch_shapes=[pltpu.VMEM((win_rows, n_res * KV_WIDTH), BF16),
                        pltpu.VMEM((win_rows, n_res * KV_WIDTH), BF16)],
        compiler_params=pltpu.CompilerParams(dimension_semantics=("parallel", "parallel", "parallel"),
                                             vmem_limit_bytes=VMEM_LIMIT),
        name=f"attn_d{dil}",
    )(*args)


FFN_CHUNKS = ((0, 1536), (1536, 1280))


def _post_kernel(*refs, dils):
    n_groups = len(dils)
    o_refs, refs = refs[:n_groups], refs[n_groups:]
    if n_groups > 1:
        lse_refs, refs = refs[:n_groups], refs[n_groups:]
        e_ref, refs = refs[0], refs[1:]
    x_ref, wo_ref, g_ref, wgu_ref, wd_ref, out_ref = refs[:6]
    scratch = list(refs[6:])
    stages, stages2 = {}, {}
    for g, dil in enumerate(dils):
        if dil > 1:
            stages[g] = scratch.pop(0)
        if dil > DEINTERLEAVE_STRIDE:
            stages2[g] = scratch.pop(0)
    tm = x_ref.shape[0]

    def token_rows(ref, g, width, plane0):
        dil = dils[g]
        if dil == 1:
            return ref[...].astype(F32)
        stage = stages[g]
        planes = [plane0 + c for c in range(width // LANES)]
        piece = lambda r, c: ref[:, r * width + c * LANES:r * width + (c + 1) * LANES].astype(F32)
        if dil <= DEINTERLEAVE_STRIDE:
            for r in range(dil):
                for c, pc in enumerate(planes):
                    stage[pc, pl.ds(r, tm // dil, stride=dil), :] = piece(r, c)
        else:
            stage2 = stages2[g]
            s0, s1 = DEINTERLEAVE_STRIDE, dil // DEINTERLEAVE_STRIDE
            for r0 in range(s0):
                for r1 in range(s1):
                    for c, pc in enumerate(planes):
                        stage2[pc, r0, pl.ds(r1, tm // dil, stride=s1), :] = piece(s0 * r1 + r0, c)
            for r0 in range(s0):
                for pc in planes:
                    stage[pc, pl.ds(r0, tm // s0, stride=s0), :] = stage2[pc, r0]
        return jnp.concatenate([stage[pc] for pc in planes], axis=1)

    if n_groups == 1:
        o = o_refs[0][...]
    else:
        lses = [token_rows(lse_refs[g], g, LANES, Q_WIDTH // LANES) for g in range(n_groups)]
        mx = functools.reduce(jnp.maximum, lses)
        ex = [jnp.exp(l - mx) for l in lses]
        inv = 1.0 / functools.reduce(jnp.add, ex)
        last = token_rows(o_refs[n_groups - 1], n_groups - 1, Q_WIDTH, 0)
        o = last
        for g in range(n_groups - 1):
            wt = ex[g] * inv
            hi = wt.astype(BF16)
            lo = (wt - hi.astype(F32)).astype(BF16)
            wexp = _dot(jnp.concatenate([hi, lo], axis=-1), e_ref[...])
            o = o + wexp * (token_rows(o_refs[g], g, Q_WIDTH, 0) - last)
        o = o.astype(BF16)

    x1 = x_ref[...] + _dot(o, wo_ref[...])
    ms = jnp.mean(x1 * x1, axis=-1, keepdims=True)
    h = (x1 * lax.rsqrt(ms + EPS) * g_ref[...]).astype(BF16)
    acc = x1
    for c0, cw in FFN_CHUNKS:
        gate = _dot(h, wgu_ref[:, c0:c0 + cw])
        up = _dot(h, wgu_ref[:, D_FF + c0:D_FF + c0 + cw])
        act = (gate * jax.nn.sigmoid(gate) * up).astype(BF16)
        acc = acc + _dot(act, wd_ref[c0:c0 + cw, :])
    out_ref[...] = acc


def _post_call(os_, dils, lses, expand, x2d, wo, gain, wgu, wd, tm=POST_ROWS):
    m = x2d.shape[0]
    n_groups = len(os_)
    row = lambda width: pl.BlockSpec((tm, width), lambda i: (i, 0))
    in_specs = [pl.BlockSpec((tm // dil, dil * Q_WIDTH), lambda i: (i, 0)) for dil in dils]
    args = list(os_)
    if n_groups > 1:
        in_specs += [pl.BlockSpec((tm // dil, dil * LANES), lambda i: (i, 0)) for dil in dils]
        in_specs += [_resident(expand.shape)]
        args += list(lses) + [expand]
    in_specs += [row(D_MODEL), _resident(wo.shape), _resident(gain.shape), _resident(wgu.shape),
                 _resident(wd.shape)]
    args += [x2d, wo, gain, wgu, wd]
    n_planes = Q_WIDTH // LANES + 1
    scratch = []
    for dil in dils:
        if dil > 1:
            scratch.append(pltpu.VMEM((n_planes, tm, LANES), F32))
        if dil > DEINTERLEAVE_STRIDE:
            scratch.append(pltpu.VMEM((n_planes, DEINTERLEAVE_STRIDE, tm // DEINTERLEAVE_STRIDE, LANES), F32))
    return pl.pallas_call(
        functools.partial(_post_kernel, dils=tuple(dils)),
        grid=(m // tm,),
        in_specs=in_specs,
        out_specs=row(D_MODEL),
        out_shape=jax.ShapeDtypeStruct((m, D_MODEL), F32),
        scratch_shapes=scratch,
        compiler_params=pltpu.CompilerParams(dimension_semantics=("parallel",),
                                             vmem_limit_bytes=VMEM_LIMIT),
        name=f"post{n_groups}",
    )(*args)


def _t5_buckets(rel):
    nb = NUM_BUCKETS // 2
    max_exact = nb // 2
    n = np.abs(rel)
    large = max_exact + (np.log(np.maximum(n, 1) / max_exact)
                         / math.log(MAX_DISTANCE / max_exact) * (nb - max_exact)).astype(np.int32)
    large = np.minimum(large, nb - 1)
    return ((rel > 0).astype(np.int32) * nb + np.where(n < max_exact, n, large)).astype(np.int32)


def _bias_table(rel_table, hw, dil):
    blk, half = ATTN_BLK, ATTN_HALF
    win = half + 2 * hw
    rel = np.arange(win)[None, :] - hw - np.arange(half)[:, None]
    band = np.abs(rel) <= hw
    kj = np.arange(win)[None, :]
    masks = []
    for first in (False, True):
        for last in (False, True):
            per_half = []
            for h in range(2):
                ok = band.copy()
                if first:
                    ok &= kj >= hw - h * half
                if last:
                    ok &= kj < hw + blk - h * half
                per_half.append(ok)
            masks.append(np.stack(per_half))
    mask = jnp.asarray(np.stack(masks))[:, None, None, :, None]
    n = win + half - 1
    diff = np.arange(n)
    diff = np.where(diff < win, diff, diff - n) - hw
    w = jnp.take(rel_table.astype(F32), jnp.asarray(_t5_buckets(dil * diff)), axis=0).T
    tb = jnp.tile(w, (1, half))[:, :half * (n - 1)].reshape(N_HEADS, half, n - 1)[:, :, :win]
    tb = tb.reshape(1, 2, 2, 1, 4, half, win)
    bias = jnp.where(mask, tb, NEG)
    return bias.reshape(4, 2, 2, 2, 4 * half, win).swapaxes(-1, -2)


def _pair_cols(w):
    lead = w.shape[:-1]
    return jnp.swapaxes(w.reshape(*lead, 2, 2, 4, HEAD_DIM), -3, -2).reshape(*lead, Q_WIDTH)


def _qkv_weight(w, n_groups):
    parts = []
    for g in range(n_groups):
        base = g * QKV_WIDTH
        parts.append(_pair_cols(w[:, base:base + Q_WIDTH]))
        parts.append(w[:, base + Q_WIDTH:base + QKV_WIDTH])
    return jnp.concatenate(parts, axis=1).astype(BF16)


def _out_weight(w):
    return jnp.swapaxes(w.reshape(2, 2, 4, HEAD_DIM, D_MODEL), 1, 2).reshape(Q_WIDTH, D_MODEL).astype(BF16)


def _block_diag_mean():
    idx = np.arange(MXU_TILE) // HEAD_DIM
    return jnp.asarray((idx[:, None] == idx[None, :]).astype(np.float32) / HEAD_DIM, BF16)


def _expand_matrix():
    col = np.arange(Q_WIDTH)
    p, r, e = col // PAIR_WIDTH, (col % PAIR_WIDTH) // LANES, (col % LANES) // HEAD_DIM
    src = e * HEAD_DIM + p * 4 + r
    m = (np.arange(LANES)[:, None] == src[None, :]).astype(np.float32)
    return jnp.asarray(np.concatenate([m, m], axis=0), BF16)


def _trunk(x, p):
    bn, s, _ = x.shape
    m = bn * s
    x2d = x.reshape(m, D_MODEL)

    q, k, v = _qkv_call(x2d, p["norm_attn"][0], p["a_w"], p["bd"], p["a_qg"], p["a_kg"], (1,))
    (o,) = _attn_call(q.reshape(bn, s, Q_WIDTH), k.reshape(bn, s, KV_WIDTH), v.reshape(bn, s, KV_WIDTH),
                      p["a_bias"], p["a_sink"], 1, A_HALF_WINDOW, False)
    x2d = _post_call([o.reshape(m, Q_WIDTH)], (1,), None, None, x2d, p["a_wo"], p["norm_ffn"][0],
                     p["wgu"][0], p["wd"][0])

    dils = tuple(d for _, d in DILATED_GROUPS)
    outs = _qkv_call(x2d, p["norm_attn"][1], p["b_w"], p["bd"], p["b_qg"], p["b_kg"], dils)
    os_, lses = [], []
    for gi, (window, dil) in enumerate(DILATED_GROUPS):
        q, k, v = outs[3 * gi:3 * gi + 3]
        ls = s // dil
        o, lse = _attn_call(q.reshape(bn, ls, dil * Q_WIDTH), k.reshape(bn, ls, dil * KV_WIDTH),
                            v.reshape(bn, ls, dil * KV_WIDTH), p["b_bias"][gi], None, dil,
                            window // (2 * dil), True)
        os_.append(o.reshape(m // dil, dil * Q_WIDTH))
        lses.append(lse.reshape(m // dil, dil * LANES))
    x2d = _post_call(os_, dils, lses, p["expand"], x2d, p["b_wo"], p["norm_ffn"][1], p["wgu"][1], p["wd"][1])
    return x2d.reshape(bn, s, D_MODEL)


def kernel(x_prompt, x_sample, rel_table, norm_attn, norm_ffn, a_w_qkv, a_q_gain, a_k_gain, a_sink, a_w_o,
           b_w_qkv, b_q_gain, b_k_gain, b_w_o, ffn_w_gate_up, ffn_w_down):
    n_grp = len(DILATED_GROUPS)
    assert norm_attn.shape == (2, D_MODEL) and norm_ffn.shape == (2, D_MODEL)
    assert a_w_qkv.shape == (1, D_MODEL, QKV_WIDTH) and b_w_qkv.shape == (1, D_MODEL, n_grp * QKV_WIDTH)
    assert ffn_w_gate_up.shape == (2, D_MODEL, 2 * D_FF) and ffn_w_down.shape == (2, D_FF, D_MODEL)
    for x in (x_prompt, x_sample):
        assert x.shape[-1] == D_MODEL and x.shape[1] % (ATTN_BLK * DILATED_GROUPS[-1][1]) == 0
        assert (x.shape[0] * x.shape[1]) % max(QKV_ROWS, POST_ROWS) == 0
    scale = HEAD_DIM ** -0.5
    half = ATTN_HALF
    sink_rows = jnp.broadcast_to(a_sink[0].astype(F32).reshape(2, 2, 1, 4, 1),
                                 (2, 2, 1, 4, half)).reshape(2, 2, 1, 4 * half)
    p = {
        "norm_attn": norm_attn.astype(F32).reshape(-1, 1, D_MODEL),
        "norm_ffn": norm_ffn.astype(F32).reshape(-1, 1, D_MODEL),
        "bd": _block_diag_mean(),
        "expand": _expand_matrix(),
        "a_w": _qkv_weight(a_w_qkv[0], 1),
        "a_qg": (jnp.tile(a_q_gain[0].astype(F32), N_HEADS) * scale).reshape(1, 1, Q_WIDTH),
        "a_kg": jnp.tile(a_k_gain[0].astype(F32), N_KV_HEADS).reshape(1, 1, KV_WIDTH),
        "a_bias": _bias_table(rel_table, A_HALF_WINDOW, 1),
        "a_sink": sink_rows,
        "a_wo": _out_weight(a_w_o[0]),
        "b_w": _qkv_weight(b_w_qkv[0], n_grp),
        "b_qg": (jnp.tile(b_q_gain[0].astype(F32), (1, N_HEADS)) * scale).reshape(n_grp, 1, Q_WIDTH),
        "b_kg": jnp.tile(b_k_gain[0].astype(F32), (1, N_KV_HEADS)).reshape(n_grp, 1, KV_WIDTH),
        "b_bias": [_bias_table(rel_table, w // (2 * d), d) for w, d in DILATED_GROUPS],
        "b_wo": _out_weight(b_w_o[0]),
        "wgu": ffn_w_gate_up.astype(BF16),
        "wd": ffn_w_down.astype(BF16),
    }
    return _trunk(x_prompt, p), _trunk(x_sample, p)
```

```python
import functools
import math

import numpy as np
import jax
import jax.numpy as jnp
from jax import lax
from jax.experimental import pallas as pl
from jax.experimental.pallas import tpu as pltpu

D_MODEL = 1024
N_HEADS = 16
N_KV_HEADS = 4
HEAD_DIM = 64
Q_WIDTH = N_HEADS * HEAD_DIM
KV_WIDTH = N_KV_HEADS * HEAD_DIM
QKV_WIDTH = Q_WIDTH + 2 * KV_WIDTH
D_FF = 2816
A_HALF_WINDOW = 128
DILATED_GROUPS = ((128, 1), (512, 4), (2048, 16))
NUM_BUCKETS = 32
MAX_DISTANCE = 1024
EPS = 1e-6
NEG = -1e30

PAIR_WIDTH = Q_WIDTH // 2
LANES = 128
MXU_TILE = 256
ATTN_BLK = 128
ATTN_HALF = 64
ATTN_STEP_TOKENS = 2048
ATTN_OFFSETS = (2, 4)
DEINTERLEAVE_STRIDE = 4
QKV_ROWS = 1024
POST_ROWS = 512
V7X_VMEM_BYTES = 64 * 1024 * 1024
VMEM_LIMIT = V7X_VMEM_BYTES * 7 // 8

BF16 = jnp.bfloat16
F32 = jnp.float32


def _dot(a, b):
    return jnp.dot(a, b, preferred_element_type=F32)


def _dot_nt(a, b):
    return lax.dot_general(a, b, (((1,), (1,)), ((), ())), preferred_element_type=F32)


def _dot_tn(a, b):
    return lax.dot_general(a, b, (((0,), (0,)), ((), ())), preferred_element_type=F32)


def _resident(shape):
    nd = len(shape)
    return pl.BlockSpec(shape, lambda *_: (0,) * nd, pipeline_mode=pl.Buffered(1))


def _qkv_kernel(x_ref, g_ref, w_ref, bd_ref, qg_ref, kg_ref, *refs, dils):
    out_refs, stages, stages2 = refs[:-2], refs[-2], refs[-1]
    tm = x_ref.shape[0]
    x = x_ref[...]
    ms = jnp.mean(x * x, axis=-1, keepdims=True)
    h = (x * lax.rsqrt(ms + EPS) * g_ref[...]).astype(BF16)
    bd = bd_ref[...]

    def emit(ref, dil, width, lo, val, slot):
        if dil == 1:
            ref[:, lo:lo + MXU_TILE] = val.astype(BF16)
            return
        stage, stage2 = stages.at[slot], stages2.at[slot]
        planes = range(MXU_TILE // LANES)
        for c in planes:
            stage[c] = val[:, c * LANES:(c + 1) * LANES]
        if dil <= DEINTERLEAVE_STRIDE:
            for r in range(dil):
                for c in planes:
                    col = r * width + lo + c * LANES
                    ref[:, col:col + LANES] = stage[c, pl.ds(r, tm // dil, stride=dil), :].astype(BF16)
            return
        s0, s1 = DEINTERLEAVE_STRIDE, dil // DEINTERLEAVE_STRIDE
        for r0 in range(s0):
            for c in planes:
                stage2[c, r0] = stage[c, pl.ds(r0, tm // s0, stride=s0), :]
        for r0 in range(s0):
            for r1 in range(s1):
                for c in planes:
                    col = (s0 * r1 + r0) * width + lo + c * LANES
                    ref[:, col:col + LANES] = stage2[c, r0, pl.ds(r1, tm // dil, stride=s1), :].astype(BF16)

    n_cols = QKV_WIDTH // MXU_TILE
    n_norm = (Q_WIDTH + KV_WIDTH) // MXU_TILE
    units = [(g, c) for g in reversed(range(len(dils))) for c in range(n_cols)]
    proj, sumsq = {}, {}

    def project(u):
        g, c = u
        lo = g * QKV_WIDTH + c * MXU_TILE
        proj[u] = _dot(h, w_ref[:, lo:lo + MXU_TILE])

    def square_sum(u):
        if u[1] < n_norm:
            t = proj[u]
            sumsq[u] = _dot((t * t).astype(BF16), bd)

    def write(i):
        g, c = u = units[i]
        q_ref, k_ref, v_ref = out_refs[3 * g:3 * g + 3]
        val = proj.pop(u)
        lo = c * MXU_TILE
        if c < n_norm:
            gain = qg_ref[g, :, lo:lo + MXU_TILE] if lo < Q_WIDTH else kg_ref[g]
            val = val * lax.rsqrt(sumsq.pop(u) + EPS) * gain
        ref, width, lo = ((q_ref, Q_WIDTH, lo) if lo < Q_WIDTH else
                          (k_ref, KV_WIDTH, 0) if c < n_norm else (v_ref, KV_WIDTH, 0))
        emit(ref, dils[g], width, lo, val, i % 2)

    n = len(units)
    for t in range(n + 2):
        if t < n:
            project(units[t])
        if 0 <= t - 1 < n:
            square_sum(units[t - 1])
        if 0 <= t - 2 < n:
            write(t - 2)


def _qkv_call(x2d, gain, w, bd, qg, kg, dils, tm=QKV_ROWS):
    m = x2d.shape[0]
    out_shape, out_specs = [], []
    for dil in dils:
        for width in (Q_WIDTH, KV_WIDTH, KV_WIDTH):
            out_shape.append(jax.ShapeDtypeStruct((m // dil, dil * width), BF16))
            out_specs.append(pl.BlockSpec((tm // dil, dil * width), lambda i: (i, 0)))
    return pl.pallas_call(
        functools.partial(_qkv_kernel, dils=tuple(dils)),
        grid=(m // tm,),
        in_specs=[pl.BlockSpec((tm, D_MODEL), lambda i: (i, 0)), _resident(gain.shape), _resident(w.shape),
                  _resident(bd.shape), _resident(qg.shape), _resident(kg.shape)],
        out_specs=out_specs,
        out_shape=out_shape,
        scratch_shapes=[pltpu.VMEM((2, MXU_TILE // LANES, tm, LANES), F32),
                        pltpu.VMEM((2, MXU_TILE // LANES, DEINTERLEAVE_STRIDE, tm // DEINTERLEAVE_STRIDE, LANES),
                                   F32)],
        compiler_params=pltpu.CompilerParams(dimension_semantics=("parallel",),
                                             vmem_limit_bytes=VMEM_LIMIT),
        name=f"qkv{len(dils)}",
    )(x2d, gain, w, bd, qg, kg)


def _attn_kernel(*refs, tq, hw, n_res, n_blocks, has_sink, has_lse):
    q_ref, kl_ref, kc_ref, kr_ref, vl_ref, vc_ref, vr_ref, bias_ref = refs[:8]
    refs = refs[8:]
    sink_ref = None
    if has_sink:
        sink_ref, refs = refs[0], refs[1:]
    o_ref, refs = refs[0], refs[1:]
    lse_ref = None
    if has_lse:
        lse_ref, refs = refs[0], refs[1:]
    kw, vw = refs

    blk = ATTN_BLK
    half = ATTN_HALF
    win = half + 2 * hw

    for dst, (l, c, r) in ((kw, (kl_ref, kc_ref, kr_ref)), (vw, (vl_ref, vc_ref, vr_ref))):
        dst[0:hw, :] = l[...]
        dst[hw:hw + tq, :] = c[...]
        dst[hw + tq:hw + tq + hw, :] = r[...]

    tile = pl.program_id(2)
    n_inner = tq // blk
    lane = lax.broadcasted_iota(jnp.int32, (1, LANES), 1)
    keep = (lane < HEAD_DIM, lane >= HEAD_DIM)
    first_half = lax.broadcasted_iota(jnp.int32, (LANES, 1), 0) < HEAD_DIM
    sub = lax.broadcasted_iota(jnp.int32, (8, 1), 0)

    chunks = [(j, u, p, h, e) for u in range(n_res) for j in range(n_inner) for p in range(2)
              for h in range(2) for e in range(2)]
    kinds, lse_rows = {}, {}
    for j in range(n_inner):
        gb = tile * n_inner + j
        kinds[j] = 2 * (gb == 0).astype(jnp.int32) + (gb == n_blocks - 1).astype(jnp.int32)
    scores_, probs_, results_ = {}, {}, {}

    def scores(c):
        j, u, p, h, e = c
        r0 = j * blk + h * half
        kp = kw[r0:r0 + win, u * KV_WIDTH + p * LANES:u * KV_WIDTH + (p + 1) * LANES]
        base = u * Q_WIDTH + p * PAIR_WIDTH
        qcat = jnp.concatenate([q_ref[r0:r0 + half, base + r * LANES:base + (r + 1) * LANES]
                                for r in range(4)], axis=0)
        k_e = jnp.where(keep[e], kp, jnp.zeros((), BF16))
        scores_[c] = _dot_nt(k_e, qcat) + bias_ref[kinds[j], p, e, h]

    def probs(c):
        j, u, p, h, e = c
        st = scores_.pop(c)
        m = jnp.max(st, axis=0, keepdims=True)
        sk = None
        if has_sink:
            sk = sink_ref[p, e]
            m = jnp.maximum(m, sk)
        probs_[c] = (jnp.exp((st - m).astype(BF16)), m, sk)

    def values(c):
        j, u, p, h, e = c
        pt, m, sk = probs_.pop(c)
        r0 = j * blk + h * half
        vp = vw[r0:r0 + win, u * KV_WIDTH + p * LANES:u * KV_WIDTH + (p + 1) * LANES]
        v_e = jnp.where(keep[e], vp, jnp.ones((), BF16))
        ot = _dot_tn(v_e, pt)
        den = ot[HEAD_DIM:HEAD_DIM + 1] if e == 0 else ot[0:1]
        if has_sink:
            den = den + jnp.exp(sk - m)
        results_[c] = (ot, 1.0 / den, (m + jnp.log(den)) if has_lse else None)
        if e == 1:
            finish(j, u, p, h)

    def finish(j, u, p, h):
        (o0, i0, l0), (o1, i1, l1) = results_.pop((j, u, p, h, 0)), results_.pop((j, u, p, h, 1))
        zeros = jnp.zeros((8, blk), F32)
        lse_lo, lse_hi = lse_rows.get((j, u), (zeros, zeros))
        r0 = j * blk + h * half
        for rp in range(2):
            cs = slice(rp * LANES, (rp + 1) * LANES)
            t = jnp.where(first_half, o0[:, cs] * i0[:, cs], o1[:, cs] * i1[:, cs])
            tt = t.T.astype(BF16)
            for s in range(2):
                col = u * Q_WIDTH + p * PAIR_WIDTH + (2 * rp + s) * LANES
                o_ref[r0:r0 + half, col:col + LANES] = tt[s * half:(s + 1) * half]
            if has_lse:
                v0, v1 = l0[:, cs], l1[:, cs]
                w0, w1 = pltpu.roll(v0, HEAD_DIM, axis=1), pltpu.roll(v1, HEAD_DIM, axis=1)
                for s in range(2):
                    mask = (sub == p * 4 + 2 * rp + s) & keep[h]
                    lse_lo = jnp.where(mask, v0 if s == h else w0, lse_lo)
                    lse_hi = jnp.where(mask, v1 if s == h else w1, lse_hi)
        lse_rows[(j, u)] = (lse_lo, lse_hi)
        if has_lse and p == 1 and h == 1:
            z = jnp.zeros((HEAD_DIM - 8, blk), F32)
            lse_ref[j * blk:(j + 1) * blk, u * LANES:(u + 1) * LANES] = (
                jnp.concatenate([lse_lo, z, lse_hi, z], axis=0).T)

    n = len(chunks)
    d1, d2 = ATTN_OFFSETS
    for t in range(n + d2):
        if t < n:
            scores(chunks[t])
        if 0 <= t - d1 < n:
            probs(chunks[t - d1])
        if 0 <= t - d2 < n:
            values(chunks[t - d2])


def _attn_call(q, k, v, bias, sink_rows, dil, hw, has_lse):
    bn, ls, _ = q.shape
    blk = ATTN_BLK
    tq = min(ATTN_STEP_TOKENS, ls)
    n_res = ATTN_STEP_TOKENS // tq
    assert ls % tq == 0 and tq % blk == 0 and tq % hw == 0 and blk % hw == 0 and dil % n_res == 0
    n_tiles = ls // tq
    ratio = tq // hw
    last_halo = ls // hw - 1

    center = lambda b, r, i: (b, i, r)
    left = lambda b, r, i: (b, jnp.maximum(i * ratio - 1, 0), r)
    right = lambda b, r, i: (b, jnp.minimum((i + 1) * ratio, last_halo), r)
    kv_specs = [pl.BlockSpec((None, hw, n_res * KV_WIDTH), left),
                pl.BlockSpec((None, tq, n_res * KV_WIDTH), center),
                pl.BlockSpec((None, hw, n_res * KV_WIDTH), right)]
    in_specs = ([pl.BlockSpec((None, tq, n_res * Q_WIDTH), center)] + kv_specs + kv_specs
                + [_resident(bias.shape)])
    args = [q, k, k, k, v, v, v, bias]
    if sink_rows is not None:
        in_specs.append(_resident(sink_rows.shape))
        args.append(sink_rows)
    out_shape = [jax.ShapeDtypeStruct(q.shape, BF16)]
    out_specs = [pl.BlockSpec((None, tq, n_res * Q_WIDTH), center)]
    if has_lse:
        out_shape.append(jax.ShapeDtypeStruct((bn, ls, dil * LANES), F32))
        out_specs.append(pl.BlockSpec((None, tq, n_res * LANES), center))
    win_rows = tq + 2 * hw
    return pl.pallas_call(
        functools.partial(_attn_kernel, tq=tq, hw=hw, n_res=n_res, n_blocks=ls // blk,
                          has_sink=sink_rows is not None, has_lse=has_lse),
        grid=(bn, dil // n_res, n_tiles),
        in_specs=in_specs,
        out_specs=out_specs,
        out_shape=out_shape,
        scratch_shapes=[pltpu.VMEM((win_rows, n_res * KV_WIDTH), BF16),
                        pltpu.VMEM((win_rows, n_res * KV_WIDTH), BF16)],
        compiler_params=pltpu.CompilerParams(dimension_semantics=("parallel", "parallel", "parallel"),
                                             vmem_limit_bytes=VMEM_LIMIT),
        name=f"attn_d{dil}",
    )(*args)


FFN_CHUNKS = ((0, 1536), (1536, 1280))


def _post_kernel(*refs, dils):
    n_groups = len(dils)
    o_refs, refs = refs[:n_groups], refs[n_groups:]
    if n_groups > 1:
        lse_refs, refs = refs[:n_groups], refs[n_groups:]
        e_ref, refs = refs[0], refs[1:]
    x_ref, wo_ref, g_ref, wgu_ref, wd_ref, out_ref = refs[:6]
    scratch = list(refs[6:])
    stages, stages2 = {}, {}
    for g, dil in enumerate(dils):
        if dil > 1:
            stages[g] = scratch.pop(0)
        if dil > DEINTERLEAVE_STRIDE:
            stages2[g] = scratch.pop(0)
    tm = x_ref.shape[0]

    def token_rows(ref, g, width, plane0):
        dil = dils[g]
        if dil == 1:
            return ref[...].astype(F32)
        stage = stages[g]
        planes = [plane0 + c for c in range(width // LANES)]
        piece = lambda r, c: ref[:, r * width + c * LANES:r * width + (c + 1) * LANES].astype(F32)
        if dil <= DEINTERLEAVE_STRIDE:
            for r in range(dil):
                for c, pc in enumerate(planes):
                    stage[pc, pl.ds(r, tm // dil, stride=dil), :] = piece(r, c)
        else:
            stage2 = stages2[g]
            s0, s1 = DEINTERLEAVE_STRIDE, dil // DEINTERLEAVE_STRIDE
            for r0 in range(s0):
                for r1 in range(s1):
                    for c, pc in enumerate(planes):
                        stage2[pc, r0, pl.ds(r1, tm // dil, stride=s1), :] = piece(s0 * r1 + r0, c)
            for r0 in range(s0):
                for pc in planes:
                    stage[pc, pl.ds(r0, tm // s0, stride=s0), :] = stage2[pc, r0]
        return jnp.concatenate([stage[pc] for pc in planes], axis=1)

    if n_groups == 1:
        o = o_refs[0][...]
    else:
        lses = [token_rows(lse_refs[g], g, LANES, Q_WIDTH // LANES) for g in range(n_groups)]
        mx = functools.reduce(jnp.maximum, lses)
        ex = [jnp.exp(l - mx) for l in lses]
        inv = 1.0 / functools.reduce(jnp.add, ex)
        last = token_rows(o_refs[n_groups - 1], n_groups - 1, Q_WIDTH, 0)
        o = last
        for g in range(n_groups - 1):
            wt = ex[g] * inv
            hi = wt.astype(BF16)
            lo = (wt - hi.astype(F32)).astype(BF16)
            wexp = _dot(jnp.concatenate([hi, lo], axis=-1), e_ref[...])
            o = o + wexp * (token_rows(o_refs[g], g, Q_WIDTH, 0) - last)
        o = o.astype(BF16)

    x1 = x_ref[...] + _dot(o, wo_ref[...])
    ms = jnp.mean(x1 * x1, axis=-1, keepdims=True)
    h = (x1 * lax.rsqrt(ms + EPS) * g_ref[...]).astype(BF16)
    acc = x1
    for c0, cw in FFN_CHUNKS:
        gate = _dot(h, wgu_ref[:, c0:c0 + cw])
        up = _dot(h, wgu_ref[:, D_FF + c0:D_FF + c0 + cw])
        act = (gate * jax.nn.sigmoid(gate) * up).astype(BF16)
        acc = acc + _dot(act, wd_ref[c0:c0 + cw, :])
    out_ref[...] = acc


def _post_call(os_, dils, lses, expand, x2d, wo, gain, wgu, wd, tm=POST_ROWS):
    m = x2d.shape[0]
    n_groups = len(os_)
    row = lambda width: pl.BlockSpec((tm, width), lambda i: (i, 0))
    in_specs = [pl.BlockSpec((tm // dil, dil * Q_WIDTH), lambda i: (i, 0)) for dil in dils]
    args = list(os_)
    if n_groups > 1:
        in_specs += [pl.BlockSpec((tm // dil, dil * LANES), lambda i: (i, 0)) for dil in dils]
        in_specs += [_resident(expand.shape)]
        args += list(lses) + [expand]
    in_specs += [row(D_MODEL), _resident(wo.shape), _resident(gain.shape), _resident(wgu.shape),
                 _resident(wd.shape)]
    args += [x2d, wo, gain, wgu, wd]
    n_planes = Q_WIDTH // LANES + 1
    scratch = []
    for dil in dils:
        if dil > 1:
            scratch.append(pltpu.VMEM((n_planes, tm, LANES), F32))
        if dil > DEINTERLEAVE_STRIDE:
            scratch.append(pltpu.VMEM((n_planes, DEINTERLEAVE_STRIDE, tm // DEINTERLEAVE_STRIDE, LANES), F32))
    return pl.pallas_call(
        functools.partial(_post_kernel, dils=tuple(dils)),
        grid=(m // tm,),
        in_specs=in_specs,
        out_specs=row(D_MODEL),
        out_shape=jax.ShapeDtypeStruct((m, D_MODEL), F32),
        scratch_shapes=scratch,
        compiler_params=pltpu.CompilerParams(dimension_semantics=("parallel",),
                                             vmem_limit_bytes=VMEM_LIMIT),
        name=f"post{n_groups}",
    )(*args)


def _t5_buckets(rel):
    nb = NUM_BUCKETS // 2
    max_exact = nb // 2
    n = np.abs(rel)
    large = max_exact + (np.log(np.maximum(n, 1) / max_exact)
                         / math.log(MAX_DISTANCE / max_exact) * (nb - max_exact)).astype(np.int32)
    large = np.minimum(large, nb - 1)
    return ((rel > 0).astype(np.int32) * nb + np.where(n < max_exact, n, large)).astype(np.int32)


def _bias_table(rel_table, hw, dil):
    blk, half = ATTN_BLK, ATTN_HALF
    win = half + 2 * hw
    rel = np.arange(win)[None, :] - hw - np.arange(half)[:, None]
    band = np.abs(rel) <= hw
    kj = np.arange(win)[None, :]
    masks = []
    for first in (False, True):
        for last in (False, True):
            per_half = []
            for h in range(2):
                ok = band.copy()
                if first:
                    ok &= kj >= hw - h * half
                if last:
                    ok &= kj < hw + blk - h * half
                per_half.append(ok)
            masks.append(np.stack(per_half))
    mask = jnp.asarray(np.stack(masks))[:, None, None, :, None]
    n = win + half - 1
    diff = np.arange(n)
    diff = np.where(diff < win, diff, diff - n) - hw
    w = jnp.take(rel_table.astype(F32), jnp.asarray(_t5_buckets(dil * diff)), axis=0).T
    tb = jnp.tile(w, (1, half))[:, :half * (n - 1)].reshape(N_HEADS, half, n - 1)[:, :, :win]
    tb = tb.reshape(1, 2, 2, 1, 4, half, win)
    bias = jnp.where(mask, tb, NEG)
    return bias.reshape(4, 2, 2, 2, 4 * half, win).swapaxes(-1, -2)


def _pair_cols(w):
    lead = w.shape[:-1]
    return jnp.swapaxes(w.reshape(*lead, 2, 2, 4, HEAD_DIM), -3, -2).reshape(*lead, Q_WIDTH)


def _qkv_weight(w, n_groups):
    parts = []
    for g in range(n_groups):
        base = g * QKV_WIDTH
        parts.append(_pair_cols(w[:, base:base + Q_WIDTH]))
        parts.append(w[:, base + Q_WIDTH:base + QKV_WIDTH])
    return jnp.concatenate(parts, axis=1).astype(BF16)


def _out_weight(w):
    return jnp.swapaxes(w.reshape(2, 2, 4, HEAD_DIM, D_MODEL), 1, 2).reshape(Q_WIDTH, D_MODEL).astype(BF16)


def _block_diag_mean():
    idx = np.arange(MXU_TILE) // HEAD_DIM
    return jnp.asarray((idx[:, None] == idx[None, :]).astype(np.float32) / HEAD_DIM, BF16)


def _expand_matrix():
    col = np.arange(Q_WIDTH)
    p, r, e = col // PAIR_WIDTH, (col % PAIR_WIDTH) // LANES, (col % LANES) // HEAD_DIM
    src = e * HEAD_DIM + p * 4 + r
    m = (np.arange(LANES)[:, None] == src[None, :]).astype(np.float32)
    return jnp.asarray(np.concatenate([m, m], axis=0), BF16)


def _trunk(x, p):
    bn, s, _ = x.shape
    m = bn * s
    x2d = x.reshape(m, D_MODEL)

    q, k, v = _qkv_call(x2d, p["norm_attn"][0], p["a_w"], p["bd"], p["a_qg"], p["a_kg"], (1,))
    (o,) = _attn_call(q.reshape(bn, s, Q_WIDTH), k.reshape(bn, s, KV_WIDTH), v.reshape(bn, s, KV_WIDTH),
                      p["a_bias"], p["a_sink"], 1, A_HALF_WINDOW, False)
    x2d = _post_call([o.reshape(m, Q_WIDTH)], (1,), None, None, x2d, p["a_wo"], p["norm_ffn"][0],
                     p["wgu"][0], p["wd"][0])

    dils = tuple(d for _, d in DILATED_GROUPS)
    outs = _qkv_call(x2d, p["norm_attn"][1], p["b_w"], p["bd"], p["b_qg"], p["b_kg"], dils)
    os_, lses = [], []
    for gi, (window, dil) in enumerate(DILATED_GROUPS):
        q, k, v = outs[3 * gi:3 * gi + 3]
        ls = s // dil
        o, lse = _attn_call(q.reshape(bn, ls, dil * Q_WIDTH), k.reshape(bn, ls, dil * KV_WIDTH),
                            v.reshape(bn, ls, dil * KV_WIDTH), p["b_bias"][gi], None, dil,
                            window // (2 * dil), True)
        os_.append(o.reshape(m // dil, dil * Q_WIDTH))
        lses.append(lse.reshape(m // dil, dil * LANES))
    x2d = _post_call(os_, dils, lses, p["expand"], x2d, p["b_wo"], p["norm_ffn"][1], p["wgu"][1], p["wd"][1])
    return x2d.reshape(bn, s, D_MODEL)


def kernel(x_prompt, x_sample, rel_table, norm_attn, norm_ffn, a_w_qkv, a_q_gain, a_k_gain, a_sink, a_w_o,
           b_w_qkv, b_q_gain, b_k_gain, b_w_o, ffn_w_gate_up, ffn_w_down):
    n_grp = len(DILATED_GROUPS)
    assert norm_attn.shape == (2, D_MODEL) and norm_ffn.shape == (2, D_MODEL)
    assert a_w_qkv.shape == (1, D_MODEL, QKV_WIDTH) and b_w_qkv.shape == (1, D_MODEL, n_grp * QKV_WIDTH)
    assert ffn_w_gate_up.shape == (2, D_MODEL, 2 * D_FF) and ffn_w_down.shape == (2, D_FF, D_MODEL)
    for x in (x_prompt, x_sample):
        assert x.shape[-1] == D_MODEL and x.shape[1] % (ATTN_BLK * DILATED_GROUPS[-1][1]) == 0
        assert (x.shape[0] * x.shape[1]) % max(QKV_ROWS, POST_ROWS) == 0
    scale = HEAD_DIM ** -0.5
    half = ATTN_HALF
    sink_rows = jnp.broadcast_to(a_sink[0].astype(F32).reshape(2, 2, 1, 4, 1),
                                 (2, 2, 1, 4, half)).reshape(2, 2, 1, 4 * half)
    p = {
        "norm_attn": norm_attn.astype(F32).reshape(-1, 1, D_MODEL),
        "norm_ffn": norm_ffn.astype(F32).reshape(-1, 1, D_MODEL),
        "bd": _block_diag_mean(),
        "expand": _expand_matrix(),
        "a_w": _qkv_weight(a_w_qkv[0], 1),
        "a_qg": (jnp.tile(a_q_gain[0].astype(F32), N_HEADS) * scale).reshape(1, 1, Q_WIDTH),
        "a_kg": jnp.tile(a_k_gain[0].astype(F32), N_KV_HEADS).reshape(1, 1, KV_WIDTH),
        "a_bias": _bias_table(rel_table, A_HALF_WINDOW, 1),
        "a_sink": sink_rows,
        "a_wo": _out_weight(a_w_o[0]),
        "b_w": _qkv_weight(b_w_qkv[0], n_grp),
        "b_qg": (jnp.tile(b_q_gain[0].astype(F32), (1, N_HEADS)) * scale).reshape(n_grp, 1, Q_WIDTH),
        "b_kg": jnp.tile(b_k_gain[0].astype(F32), (1, N_KV_HEADS)).reshape(n_grp, 1, KV_WIDTH),
        "b_bias": [_bias_table(rel_table, w // (2 * d), d) for w, d in DILATED_GROUPS],
        "b_wo": _out_weight(b_w_o[0]),
        "wgu": ffn_w_gate_up.astype(BF16),
        "wd": ffn_w_down.astype(BF16),
    }
    return _trunk(x_prompt, p), _trunk(x_sample, p)
```

```python
import functools
import math

import numpy as np
import jax
import jax.numpy as jnp
from jax import lax
from jax.experimental import pallas as pl
from jax.experimental.pallas import tpu as pltpu

D_MODEL = 1024
N_HEADS = 16
N_KV_HEADS = 4
HEAD_DIM = 64
Q_WIDTH = N_HEADS * HEAD_DIM
KV_WIDTH = N_KV_HEADS * HEAD_DIM
QKV_WIDTH = Q_WIDTH + 2 * KV_WIDTH
D_FF = 2816
A_HALF_WINDOW = 128
DILATED_GROUPS = ((128, 1), (512, 4), (2048, 16))
NUM_BUCKETS = 32
MAX_DISTANCE = 1024
EPS = 1e-6
NEG = -1e30

PAIR_WIDTH = Q_WIDTH // 2
LANES = 128
MXU_TILE = 256
ATTN_BLK = 128
ATTN_HALF = 64
ATTN_STEP_TOKENS = 2048
ATTN_OFFSETS = (2, 4)
DEINTERLEAVE_STRIDE = 4
QKV_OFFSETS = (2, 4)
QKV_ROWS = 1024
POST_ROWS = 512
V7X_VMEM_BYTES = 64 * 1024 * 1024
VMEM_LIMIT = V7X_VMEM_BYTES * 7 // 8

BF16 = jnp.bfloat16
F32 = jnp.float32


def _dot(a, b):
    return jnp.dot(a, b, preferred_element_type=F32)


def _dot_nt(a, b):
    return lax.dot_general(a, b, (((1,), (1,)), ((), ())), preferred_element_type=F32)


def _dot_tn(a, b):
    return lax.dot_general(a, b, (((0,), (0,)), ((), ())), preferred_element_type=F32)


def _resident(shape):
    nd = len(shape)
    return pl.BlockSpec(shape, lambda *_: (0,) * nd, pipeline_mode=pl.Buffered(1))


def _qkv_kernel(x_ref, g_ref, w_ref, bd_ref, qg_ref, kg_ref, *refs, dils):
    out_refs, stages, stages2 = refs[:-2], refs[-2], refs[-1]
    tm = x_ref.shape[0]
    x = x_ref[...]
    ms = jnp.mean(x * x, axis=-1, keepdims=True)
    h = (x * lax.rsqrt(ms + EPS) * g_ref[...]).astype(BF16)
    bd = bd_ref[...]

    def emit(ref, dil, width, lo, val, slot):
        if dil == 1:
            ref[:, lo:lo + MXU_TILE] = val.astype(BF16)
            return
        stage, stage2 = stages.at[slot], stages2.at[slot]
        planes = range(MXU_TILE // LANES)
        for c in planes:
            stage[c] = val[:, c * LANES:(c + 1) * LANES]
        if dil <= DEINTERLEAVE_STRIDE:
            for r in range(dil):
                for c in planes:
                    col = r * width + lo + c * LANES
                    ref[:, col:col + LANES] = stage[c, pl.ds(r, tm // dil, stride=dil), :].astype(BF16)
            return
        s0, s1 = DEINTERLEAVE_STRIDE, dil // DEINTERLEAVE_STRIDE
        for r0 in range(s0):
            for c in planes:
                stage2[c, r0] = stage[c, pl.ds(r0, tm // s0, stride=s0), :]
        for r0 in range(s0):
            for r1 in range(s1):
                for c in planes:
                    col = (s0 * r1 + r0) * width + lo + c * LANES
                    ref[:, col:col + LANES] = stage2[c, r0, pl.ds(r1, tm // dil, stride=s1), :].astype(BF16)

    n_cols = QKV_WIDTH // MXU_TILE
    n_norm = (Q_WIDTH + KV_WIDTH) // MXU_TILE
    units = [(g, c) for g in reversed(range(len(dils))) for c in range(n_cols)]
    proj, sumsq = {}, {}

    def project(u):
        g, c = u
        lo = g * QKV_WIDTH + c * MXU_TILE
        proj[u] = _dot(h, w_ref[:, lo:lo + MXU_TILE])

    def square_sum(u):
        if u[1] < n_norm:
            t = proj[u]
            sumsq[u] = _dot((t * t).astype(BF16), bd)

    def write(i):
        g, c = u = units[i]
        q_ref, k_ref, v_ref = out_refs[3 * g:3 * g + 3]
        val = proj.pop(u)
        lo = c * MXU_TILE
        if c < n_norm:
            gain = qg_ref[g, :, lo:lo + MXU_TILE] if lo < Q_WIDTH else kg_ref[g]
            val = val * lax.rsqrt(sumsq.pop(u) + EPS) * gain
        ref, width, lo = ((q_ref, Q_WIDTH, lo) if lo < Q_WIDTH else
                          (k_ref, KV_WIDTH, 0) if c < n_norm else (v_ref, KV_WIDTH, 0))
        emit(ref, dils[g], width, lo, val, i % 2)

    n = len(units)
    d1, d2 = QKV_OFFSETS
    for t in range(n + d2):
        if t < n:
            project(units[t])
        if 0 <= t - d1 < n:
            square_sum(units[t - d1])
        if 0 <= t - d2 < n:
            write(t - d2)


def _qkv_call(x2d, gain, w, bd, qg, kg, dils, tm=QKV_ROWS):
    m = x2d.shape[0]
    out_shape, out_specs = [], []
    for dil in dils:
        for width in (Q_WIDTH, KV_WIDTH, KV_WIDTH):
            out_shape.append(jax.ShapeDtypeStruct((m // dil, dil * width), BF16))
            out_specs.append(pl.BlockSpec((tm // dil, dil * width), lambda i: (i, 0)))
    return pl.pallas_call(
        functools.partial(_qkv_kernel, dils=tuple(dils)),
        grid=(m // tm,),
        in_specs=[pl.BlockSpec((tm, D_MODEL), lambda i: (i, 0)), _resident(gain.shape), _resident(w.shape),
                  _resident(bd.shape), _resident(qg.shape), _resident(kg.shape)],
        out_specs=out_specs,
        out_shape=out_shape,
        scratch_shapes=[pltpu.VMEM((2, MXU_TILE // LANES, tm, LANES), F32),
                        pltpu.VMEM((2, MXU_TILE // LANES, DEINTERLEAVE_STRIDE, tm // DEINTERLEAVE_STRIDE, LANES),
                                   F32)],
        compiler_params=pltpu.CompilerParams(dimension_semantics=("parallel",),
                                             vmem_limit_bytes=VMEM_LIMIT),
        name=f"qkv{len(dils)}",
    )(x2d, gain, w, bd, qg, kg)


def _attn_kernel(*refs, tq, hw, n_res, n_blocks, has_sink, has_lse):
    q_ref, kl_ref, kc_ref, kr_ref, vl_ref, vc_ref, vr_ref, bias_ref = refs[:8]
    refs = refs[8:]
    sink_ref = None
    if has_sink:
        sink_ref, refs = refs[0], refs[1:]
    o_ref, refs = refs[0], refs[1:]
    lse_ref = None
    if has_lse:
        lse_ref, refs = refs[0], refs[1:]
    kw, vw = refs

    blk = ATTN_BLK
    half = ATTN_HALF
    win = half + 2 * hw

    for dst, (l, c, r) in ((kw, (kl_ref, kc_ref, kr_ref)), (vw, (vl_ref, vc_ref, vr_ref))):
        dst[0:hw, :] = l[...]
        dst[hw:hw + tq, :] = c[...]
        dst[hw + tq:hw + tq + hw, :] = r[...]

    tile = pl.program_id(2)
    n_inner = tq // blk
    lane = lax.broadcasted_iota(jnp.int32, (1, LANES), 1)
    keep = (lane < HEAD_DIM, lane >= HEAD_DIM)
    first_half = lax.broadcasted_iota(jnp.int32, (LANES, 1), 0) < HEAD_DIM
    sub = lax.broadcasted_iota(jnp.int32, (8, 1), 0)

    chunks = [(j, u, p, h, e) for u in range(n_res) for j in range(n_inner) for p in range(2)
              for h in range(2) for e in range(2)]
    kinds, lse_rows = {}, {}
    for j in range(n_inner):
        gb = tile * n_inner + j
        kinds[j] = 2 * (gb == 0).astype(jnp.int32) + (gb == n_blocks - 1).astype(jnp.int32)
    scores_, probs_, results_ = {}, {}, {}

    def scores(c):
        j, u, p, h, e = c
        r0 = j * blk + h * half
        kp = kw[r0:r0 + win, u * KV_WIDTH + p * LANES:u * KV_WIDTH + (p + 1) * LANES]
        base = u * Q_WIDTH + p * PAIR_WIDTH
        qcat = jnp.concatenate([q_ref[r0:r0 + half, base + r * LANES:base + (r + 1) * LANES]
                                for r in range(4)], axis=0)
        k_e = jnp.where(keep[e], kp, jnp.zeros((), BF16))
        scores_[c] = _dot_nt(k_e, qcat) + bias_ref[kinds[j], p, e, h]

    def probs(c):
        j, u, p, h, e = c
        st = scores_.pop(c)
        m = jnp.max(st, axis=0, keepdims=True)
        sk = None
        if has_sink:
            sk = sink_ref[p, e]
            m = jnp.maximum(m, sk)
        probs_[c] = (jnp.exp((st - m).astype(BF16)), m, sk)

    def values(c):
        j, u, p, h, e = c
        pt, m, sk = probs_.pop(c)
        r0 = j * blk + h * half
        vp = vw[r0:r0 + win, u * KV_WIDTH + p * LANES:u * KV_WIDTH + (p + 1) * LANES]
        v_e = jnp.where(keep[e], vp, jnp.ones((), BF16))
        ot = _dot_tn(v_e, pt)
        den = ot[HEAD_DIM:HEAD_DIM + 1] if e == 0 else ot[0:1]
        if has_sink:
            den = den + jnp.exp(sk - m)
        results_[c] = (ot, 1.0 / den, (m + jnp.log(den)) if has_lse else None)
        if e == 1:
            finish(j, u, p, h)

    def finish(j, u, p, h):
        (o0, i0, l0), (o1, i1, l1) = results_.pop((j, u, p, h, 0)), results_.pop((j, u, p, h, 1))
        zeros = jnp.zeros((8, blk), F32)
        lse_lo, lse_hi = lse_rows.get((j, u), (zeros, zeros))
        r0 = j * blk + h * half
        for rp in range(2):
            cs = slice(rp * LANES, (rp + 1) * LANES)
            t = jnp.where(first_half, o0[:, cs] * i0[:, cs], o1[:, cs] * i1[:, cs])
            tt = t.T.astype(BF16)
            for s in range(2):
                col = u * Q_WIDTH + p * PAIR_WIDTH + (2 * rp + s) * LANES
                o_ref[r0:r0 + half, col:col + LANES] = tt[s * half:(s + 1) * half]
            if has_lse:
                v0, v1 = l0[:, cs], l1[:, cs]
                w0, w1 = pltpu.roll(v0, HEAD_DIM, axis=1), pltpu.roll(v1, HEAD_DIM, axis=1)
                for s in range(2):
                    mask = (sub == p * 4 + 2 * rp + s) & keep[h]
                    lse_lo = jnp.where(mask, v0 if s == h else w0, lse_lo)
                    lse_hi = jnp.where(mask, v1 if s == h else w1, lse_hi)
        lse_rows[(j, u)] = (lse_lo, lse_hi)
        if has_lse and p == 1 and h == 1:
            z = jnp.zeros((HEAD_DIM - 8, blk), F32)
            lse_ref[j * blk:(j + 1) * blk, u * LANES:(u + 1) * LANES] = (
                jnp.concatenate([lse_lo, z, lse_hi, z], axis=0).T)

    n = len(chunks)
    d1, d2 = ATTN_OFFSETS
    for t in range(n + d2):
        if t < n:
            scores(chunks[t])
        if 0 <= t - d1 < n:
            probs(chunks[t - d1])
        if 0 <= t - d2 < n:
            values(chunks[t - d2])


def _attn_call(q, k, v, bias, sink_rows, dil, hw, has_lse):
    bn, ls, _ = q.shape
    blk = ATTN_BLK
    tq = min(ATTN_STEP_TOKENS, ls)
    n_res = ATTN_STEP_TOKENS // tq
    assert ls % tq == 0 and tq % blk == 0 and tq % hw == 0 and blk % hw == 0 and dil % n_res == 0
    n_tiles = ls // tq
    ratio = tq // hw
    last_halo = ls // hw - 1

    center = lambda b, r, i: (b, i, r)
    left = lambda b, r, i: (b, jnp.maximum(i * ratio - 1, 0), r)
    right = lambda b, r, i: (b, jnp.minimum((i + 1) * ratio, last_halo), r)
    kv_specs = [pl.BlockSpec((None, hw, n_res * KV_WIDTH), left),
                pl.BlockSpec((None, tq, n_res * KV_WIDTH), center),
                pl.BlockSpec((None, hw, n_res * KV_WIDTH), right)]
    in_specs = ([pl.BlockSpec((None, tq, n_res * Q_WIDTH), center)] + kv_specs + kv_specs
                + [_resident(bias.shape)])
    args = [q, k, k, k, v, v, v, bias]
    if sink_rows is not None:
        in_specs.append(_resident(sink_rows.shape))
        args.append(sink_rows)
    out_shape = [jax.ShapeDtypeStruct(q.shape, BF16)]
    out_specs = [pl.BlockSpec((None, tq, n_res * Q_WIDTH), center)]
    if has_lse:
        out_shape.append(jax.ShapeDtypeStruct((bn, ls, dil * LANES), F32))
        out_specs.append(pl.BlockSpec((None, tq, n_res * LANES), center))
    win_rows = tq + 2 * hw
    return pl.pallas_call(
        functools.partial(_attn_kernel, tq=tq, hw=hw, n_res=n_res, n_blocks=ls // blk,
                          has_sink=sink_rows is not None, has_lse=has_lse),
        grid=(bn, dil // n_res, n_tiles),
        in_specs=in_specs,
        out_specs=out_specs,
        out_shape=out_shape,
        scratch_shapes=[pltpu.VMEM((win_rows, n_res * KV_WIDTH), BF16),
                        pltpu.VMEM((win_rows, n_res * KV_WIDTH), BF16)],
        compiler_params=pltpu.CompilerParams(dimension_semantics=("parallel", "parallel", "parallel"),
                                             vmem_limit_bytes=VMEM_LIMIT),
        name=f"attn_d{dil}",
    )(*args)


FFN_CHUNKS = ((0, 1536), (1536, 1280))


def _post_kernel(*refs, dils):
    n_groups = len(dils)
    o_refs, refs = refs[:n_groups], refs[n_groups:]
    if n_groups > 1:
        lse_refs, refs = refs[:n_groups], refs[n_groups:]
        e_ref, refs = refs[0], refs[1:]
    x_ref, wo_ref, g_ref, wgu_ref, wd_ref, out_ref = refs[:6]
    scratch = list(refs[6:])
    stages, stages2 = {}, {}
    for g, dil in enumerate(dils):
        if dil > 1:
            stages[g] = scratch.pop(0)
        if dil > DEINTERLEAVE_STRIDE:
            stages2[g] = scratch.pop(0)
    tm = x_ref.shape[0]

    def token_rows(ref, g, width, plane0):
        dil = dils[g]
        if dil == 1:
            return ref[...].astype(F32)
        stage = stages[g]
        planes = [plane0 + c for c in range(width // LANES)]
        piece = lambda r, c: ref[:, r * width + c * LANES:r * width + (c + 1) * LANES].astype(F32)
        if dil <= DEINTERLEAVE_STRIDE:
            for r in range(dil):
                for c, pc in enumerate(planes):
                    stage[pc, pl.ds(r, tm // dil, stride=dil), :] = piece(r, c)
        else:
            stage2 = stages2[g]
            s0, s1 = DEINTERLEAVE_STRIDE, dil // DEINTERLEAVE_STRIDE
            for r0 in range(s0):
                for r1 in range(s1):
                    for c, pc in enumerate(planes):
                        stage2[pc, r0, pl.ds(r1, tm // dil, stride=s1), :] = piece(s0 * r1 + r0, c)
            for r0 in range(s0):
                for pc in planes:
                    stage[pc, pl.ds(r0, tm // s0, stride=s0), :] = stage2[pc, r0]
        return jnp.concatenate([stage[pc] for pc in planes], axis=1)

    if n_groups == 1:
        o = o_refs[0][...]
    else:
        lses = [token_rows(lse_refs[g], g, LANES, Q_WIDTH // LANES) for g in range(n_groups)]
        mx = functools.reduce(jnp.maximum, lses)
        ex = [jnp.exp(l - mx) for l in lses]
        inv = 1.0 / functools.reduce(jnp.add, ex)
        last = token_rows(o_refs[0], 0, Q_WIDTH, 0)
        o = last
        for g in range(1, n_groups):
            wt = ex[g] * inv
            hi = wt.astype(BF16)
            lo = (wt - hi.astype(F32)).astype(BF16)
            wexp = _dot(jnp.concatenate([hi, lo], axis=-1), e_ref[...])
            o = o + wexp * (token_rows(o_refs[g], g, Q_WIDTH, 0) - last)
        o = o.astype(BF16)

    x1 = x_ref[...] + _dot(o, wo_ref[...])
    ms = jnp.mean(x1 * x1, axis=-1, keepdims=True)
    h = (x1 * lax.rsqrt(ms + EPS) * g_ref[...]).astype(BF16)
    acc = x1
    for c0, cw in FFN_CHUNKS:
        gate = _dot(h, wgu_ref[:, c0:c0 + cw])
        up = _dot(h, wgu_ref[:, D_FF + c0:D_FF + c0 + cw])
        act = (gate * jax.nn.sigmoid(gate) * up).astype(BF16)
        acc = acc + _dot(act, wd_ref[c0:c0 + cw, :])
    out_ref[...] = acc


def _post_call(os_, dils, lses, expand, x2d, wo, gain, wgu, wd, tm=POST_ROWS):
    m = x2d.shape[0]
    n_groups = len(os_)
    row = lambda width: pl.BlockSpec((tm, width), lambda i: (i, 0))
    in_specs = [pl.BlockSpec((tm // dil, dil * Q_WIDTH), lambda i: (i, 0)) for dil in dils]
    args = list(os_)
    if n_groups > 1:
        in_specs += [pl.BlockSpec((tm // dil, dil * LANES), lambda i: (i, 0)) for dil in dils]
        in_specs += [_resident(expand.shape)]
        args += list(lses) + [expand]
    in_specs += [row(D_MODEL), _resident(wo.shape), _resident(gain.shape), _resident(wgu.shape),
                 _resident(wd.shape)]
    args += [x2d, wo, gain, wgu, wd]
    n_planes = Q_WIDTH // LANES + 1
    scratch = []
    for dil in dils:
        if dil > 1:
            scratch.append(pltpu.VMEM((n_planes, tm, LANES), F32))
        if dil > DEINTERLEAVE_STRIDE:
            scratch.append(pltpu.VMEM((n_planes, DEINTERLEAVE_STRIDE, tm // DEINTERLEAVE_STRIDE, LANES), F32))
    return pl.pallas_call(
        functools.partial(_post_kernel, dils=tuple(dils)),
        grid=(m // tm,),
        in_specs=in_specs,
        out_specs=row(D_MODEL),
        out_shape=jax.ShapeDtypeStruct((m, D_MODEL), F32),
        scratch_shapes=scratch,
        compiler_params=pltpu.CompilerParams(dimension_semantics=("parallel",),
                                             vmem_limit_bytes=VMEM_LIMIT),
        name=f"post{n_groups}",
    )(*args)


def _t5_buckets(rel):
    nb = NUM_BUCKETS // 2
    max_exact = nb // 2
    n = np.abs(rel)
    large = max_exact + (np.log(np.maximum(n, 1) / max_exact)
                         / math.log(MAX_DISTANCE / max_exact) * (nb - max_exact)).astype(np.int32)
    large = np.minimum(large, nb - 1)
    return ((rel > 0).astype(np.int32) * nb + np.where(n < max_exact, n, large)).astype(np.int32)


def _bias_table(rel_table, hw, dil):
    blk, half = ATTN_BLK, ATTN_HALF
    win = half + 2 * hw
    rel = np.arange(win)[None, :] - hw - np.arange(half)[:, None]
    band = np.abs(rel) <= hw
    kj = np.arange(win)[None, :]
    masks = []
    for first in (False, True):
        for last in (False, True):
            per_half = []
            for h in range(2):
                ok = band.copy()
                if first:
                    ok &= kj >= hw - h * half
                if last:
                    ok &= kj < hw + blk - h * half
                per_half.append(ok)
            masks.append(np.stack(per_half))
    mask = jnp.asarray(np.stack(masks))[:, None, None, :, None]
    n = win + half - 1
    diff = np.arange(n)
    diff = np.where(diff < win, diff, diff - n) - hw
    w = jnp.take(rel_table.astype(F32), jnp.asarray(_t5_buckets(dil * diff)), axis=0).T
    tb = jnp.tile(w, (1, half))[:, :half * (n - 1)].reshape(N_HEADS, half, n - 1)[:, :, :win]
    tb = tb.reshape(1, 2, 2, 1, 4, half, win)
    bias = jnp.where(mask, tb, NEG)
    return bias.reshape(4, 2, 2, 2, 4 * half, win).swapaxes(-1, -2)


def _pair_cols(w):
    lead = w.shape[:-1]
    return jnp.swapaxes(w.reshape(*lead, 2, 2, 4, HEAD_DIM), -3, -2).reshape(*lead, Q_WIDTH)


def _qkv_weight(w, n_groups):
    parts = []
    for g in range(n_groups):
        base = g * QKV_WIDTH
        parts.append(_pair_cols(w[:, base:base + Q_WIDTH]))
        parts.append(w[:, base + Q_WIDTH:base + QKV_WIDTH])
    return jnp.concatenate(parts, axis=1).astype(BF16)


def _out_weight(w):
    return jnp.swapaxes(w.reshape(2, 2, 4, HEAD_DIM, D_MODEL), 1, 2).reshape(Q_WIDTH, D_MODEL).astype(BF16)


def _block_diag_mean():
    idx = np.arange(MXU_TILE) // HEAD_DIM
    return jnp.asarray((idx[:, None] == idx[None, :]).astype(np.float32) / HEAD_DIM, BF16)


def _expand_matrix():
    col = np.arange(Q_WIDTH)
    p, r, e = col // PAIR_WIDTH, (col % PAIR_WIDTH) // LANES, (col % LANES) // HEAD_DIM
    src = e * HEAD_DIM + p * 4 + r
    m = (np.arange(LANES)[:, None] == src[None, :]).astype(np.float32)
    return jnp.asarray(np.concatenate([m, m], axis=0), BF16)


def _trunk(x, p):
    bn, s, _ = x.shape
    m = bn * s
    x2d = x.reshape(m, D_MODEL)

    q, k, v = _qkv_call(x2d, p["norm_attn"][0], p["a_w"], p["bd"], p["a_qg"], p["a_kg"], (1,))
    (o,) = _attn_call(q.reshape(bn, s, Q_WIDTH), k.reshape(bn, s, KV_WIDTH), v.reshape(bn, s, KV_WIDTH),
                      p["a_bias"], p["a_sink"], 1, A_HALF_WINDOW, False)
    x2d = _post_call([o.reshape(m, Q_WIDTH)], (1,), None, None, x2d, p["a_wo"], p["norm_ffn"][0],
                     p["wgu"][0], p["wd"][0])

    dils = tuple(d for _, d in DILATED_GROUPS)
    outs = _qkv_call(x2d, p["norm_attn"][1], p["b_w"], p["bd"], p["b_qg"], p["b_kg"], dils)
    os_, lses = [], []
    for gi, (window, dil) in enumerate(DILATED_GROUPS):
        q, k, v = outs[3 * gi:3 * gi + 3]
        ls = s // dil
        o, lse = _attn_call(q.reshape(bn, ls, dil * Q_WIDTH), k.reshape(bn, ls, dil * KV_WIDTH),
                            v.reshape(bn, ls, dil * KV_WIDTH), p["b_bias"][gi], None, dil,
                            window // (2 * dil), True)
        os_.append(o.reshape(m // dil, dil * Q_WIDTH))
        lses.append(lse.reshape(m // dil, dil * LANES))
    x2d = _post_call(os_, dils, lses, p["expand"], x2d, p["b_wo"], p["norm_ffn"][1], p["wgu"][1], p["wd"][1])
    return x2d.reshape(bn, s, D_MODEL)


def kernel(x_prompt, x_sample, rel_table, norm_attn, norm_ffn, a_w_qkv, a_q_gain, a_k_gain, a_sink, a_w_o,
           b_w_qkv, b_q_gain, b_k_gain, b_w_o, ffn_w_gate_up, ffn_w_down):
    n_grp = len(DILATED_GROUPS)
    assert norm_attn.shape == (2, D_MODEL) and norm_ffn.shape == (2, D_MODEL)
    assert a_w_qkv.shape == (1, D_MODEL, QKV_WIDTH) and b_w_qkv.shape == (1, D_MODEL, n_grp * QKV_WIDTH)
    assert ffn_w_gate_up.shape == (2, D_MODEL, 2 * D_FF) and ffn_w_down.shape == (2, D_FF, D_MODEL)
    for x in (x_prompt, x_sample):
        assert x.shape[-1] == D_MODEL and x.shape[1] % (ATTN_BLK * DILATED_GROUPS[-1][1]) == 0
        assert (x.shape[0] * x.shape[1]) % max(QKV_ROWS, POST_ROWS) == 0
    scale = HEAD_DIM ** -0.5
    half = ATTN_HALF
    sink_rows = jnp.broadcast_to(a_sink[0].astype(F32).reshape(2, 2, 1, 4, 1),
                                 (2, 2, 1, 4, half)).reshape(2, 2, 1, 4 * half)
    p = {
        "norm_attn": norm_attn.astype(F32).reshape(-1, 1, D_MODEL),
        "norm_ffn": norm_ffn.astype(F32).reshape(-1, 1, D_MODEL),
        "bd": _block_diag_mean(),
        "expand": _expand_matrix(),
        "a_w": _qkv_weight(a_w_qkv[0], 1),
        "a_qg": (jnp.tile(a_q_gain[0].astype(F32), N_HEADS) * scale).reshape(1, 1, Q_WIDTH),
        "a_kg": jnp.tile(a_k_gain[0].astype(F32), N_KV_HEADS).reshape(1, 1, KV_WIDTH),
        "a_bias": _bias_table(rel_table, A_HALF_WINDOW, 1),
        "a_sink": sink_rows,
        "a_wo": _out_weight(a_w_o[0]),
        "b_w": _qkv_weight(b_w_qkv[0], n_grp),
        "b_qg": (jnp.tile(b_q_gain[0].astype(F32), (1, N_HEADS)) * scale).reshape(n_grp, 1, Q_WIDTH),
        "b_kg": jnp.tile(b_k_gain[0].astype(F32), (1, N_KV_HEADS)).reshape(n_grp, 1, KV_WIDTH),
        "b_bias": [_bias_table(rel_table, w // (2 * d), d) for w, d in DILATED_GROUPS],
        "b_wo": _out_weight(b_w_o[0]),
        "wgu": ffn_w_gate_up.astype(BF16),
        "wd": ffn_w_down.astype(BF16),
    }
    return _trunk(x_prompt, p), _trunk(x_sample, p)
```
